```python
import jax, jax.numpy as jnp
from jax import lax
import numpy as np

D_MODEL = 1024
BATCH = 32
SEQ = 2048
DEPTH = 1
DEC_BATCH = 4
DEC_SEQ = 8192
PAST_LEN = 128

A_HEADS = 4
A_DK = 128
A_DV = 128
A_FW = A_HEADS * A_DK
A_VW = A_HEADS * A_DV
A_CHUNK = 64
B_GROUPS = 4
B_GC = 128
B_W = B_GROUPS * B_GC
B_CHUNK = 128
P_HEADS = 8
P_NKEYS = 128
P_DKEY = 128
P_TOPK = 16
P_N = P_NKEYS * P_NKEYS
P_BLOCK = 128
EPS = 1e-6
IN_SPLITS = (A_FW, A_FW, A_FW, A_VW, A_VW, B_W, B_W, D_MODEL, D_MODEL)
IN_COLS = sum(IN_SPLITS)

kernel_name = "hgrn2_sgu_peer_parallel_encoder"


def rmsnorm(x, g):
    xf = x.astype(jnp.float32)
    r = xf * lax.rsqrt(jnp.mean(xf * xf, axis=-1, keepdims=True) + EPS)
    return (r * g.astype(jnp.float32)).astype(x.dtype)


def gla_scan(q, k, v, logf):
    b, h, l, dk = q.shape
    dv = v.shape[-1]
    n = l // A_CHUNK
    q = q.reshape(b, h, n, A_CHUNK, dk).astype(jnp.float32)
    k = k.reshape(b, h, n, A_CHUNK, dk).astype(jnp.float32)
    v = v.reshape(b, h, n, A_CHUNK, dv).astype(jnp.float32)
    g = jnp.cumsum(logf.reshape(b, h, n, A_CHUNK, dk).astype(jnp.float32), axis=3)
    g_last = g[:, :, :, -1:, :]
    q_dec = q * jnp.exp(g)
    k_inv = k * jnp.exp(-g)
    k_to_end = k * jnp.exp(g_last - g)
    mask = jnp.tril(jnp.ones((A_CHUNK, A_CHUNK), dtype=bool))
    att = jnp.where(mask, jnp.einsum('bhnti,bhnsi->bhnts', q_dec, k_inv), 0.0)
    o_intra = jnp.einsum('bhnts,bhnsv->bhntv', att, v)
    ds = jnp.einsum('bhnsi,bhnsv->bhniv', k_to_end, v)
    decay = jnp.exp(g_last[:, :, :, 0, :])

    def step(s, inp):
        d, dsc = inp
        return d[..., None] * s + dsc, s

    s0 = jnp.zeros((b, h, dk, dv), jnp.float32)
    _, s_prev = lax.scan(step, s0, (jnp.moveaxis(decay, 2, 0), jnp.moveaxis(ds, 2, 0)))
    s_prev = jnp.moveaxis(s_prev, 0, 2)
    o_inter = jnp.einsum('bhnti,bhniv->bhntv', q_dec, s_prev)
    return (o_intra + o_inter).reshape(b, h, l, dv)


def layer_lower_bound(lb_param, layer):
    lb_all = jnp.cumsum(jax.nn.softmax(lb_param.astype(jnp.float32), axis=0), axis=0)
    return lb_all[layer]


def hgrn2_branch(q_raw, ff_raw, fb_raw, i_raw, og_raw, lb_f, lb_b, gn):
    b, l, _ = q_raw.shape

    def heads(t, d):
        return t.reshape(b, l, A_HEADS, d).transpose(0, 2, 1, 3)

    q = jax.nn.silu(heads(q_raw, A_DK).astype(jnp.float32))
    v = heads(i_raw, A_DV)
    lbf = lb_f.reshape(A_HEADS, 1, A_DK)
    lbb = lb_b.reshape(A_HEADS, 1, A_DK)
    f_f = lbf + (1.0 - lbf) * jax.nn.sigmoid(heads(ff_raw, A_DK).astype(jnp.float32))
    f_b = lbb + (1.0 - lbb) * jax.nn.sigmoid(heads(fb_raw, A_DK).astype(jnp.float32))
    o_fwd = gla_scan(q, 1.0 - f_f, v, jnp.log(f_f))
    o_bwd = jnp.flip(gla_scan(jnp.flip(q, 2), jnp.flip(1.0 - f_b, 2), jnp.flip(v, 2),
                              jnp.flip(jnp.log(f_b), 2)), 2)
    o = o_fwd + o_bwd
    o = o * lax.rsqrt(jnp.mean(o * o, axis=-1, keepdims=True) + EPS) * gn.astype(jnp.float32)
    o = o.transpose(0, 2, 1, 3).reshape(b, l, A_VW)
    return (o * jax.nn.silu(og_raw.astype(jnp.float32))).astype(q_raw.dtype)


def sgu_branch(u_raw, v_raw, w_sp, b_sp, vg):
    b, l, _ = u_raw.shape
    n = l // B_CHUNK
    u = jax.nn.gelu(u_raw)
    v = rmsnorm(jax.nn.gelu(v_raw), vg).reshape(b, n, B_CHUNK, B_GROUPS, B_GC)
    mixed = jnp.einsum('gts,bnsgc->bntgc', w_sp, v) + b_sp.T[None, None, :, :, None]
    return u * mixed.reshape(b, l, B_W)


def peer_layer(x, w_pq, pkeys, pu, pv):
    b, l, d = x.shape
    xt = x.reshape((b * l) // P_BLOCK, P_BLOCK, d)

    def block(xb):
        qh = (xb @ w_pq).reshape(P_BLOCK, P_HEADS, 2, P_DKEY)
        sc = jnp.einsum('thpk,hpnk->thpn', qh, pkeys).astype(jnp.float32)
        s_top, i_top = lax.top_k(sc, P_TOPK)
        cand = s_top[:, :, 0, :, None] + s_top[:, :, 1, None, :]
        cand_idx = i_top[:, :, 0, :, None] * P_NKEYS + i_top[:, :, 1, None, :]
        best, pos = lax.top_k(cand.reshape(P_BLOCK, P_HEADS, P_TOPK * P_TOPK), P_TOPK)
        eidx = jnp.take_along_axis(cand_idx.reshape(P_BLOCK, P_HEADS, P_TOPK * P_TOPK), pos, axis=-1)
        gate = jax.nn.softmax(best, axis=-1)
        ue = jnp.take(pu, eidx, axis=0)
        ve = jnp.take(pv, eidx, axis=0)
        act = jax.nn.gelu(jnp.einsum('thkd,td->thk', ue, xb).astype(jnp.float32))
        return jnp.einsum('thk,thkd->td', (gate * act).astype(ve.dtype), ve)

    return lax.map(block, xt).reshape(b, l, d)


def run_trunk(x, norm1_g, w_in, lb_fwd, lb_bwd, gn_a, vnorm_g, w_sp, b_sp, w_oa, w_ob, w_out,
              norm2_g, w_pq, peer_keys, peer_u, peer_v, norm_f):
    offs = list(np.cumsum(IN_SPLITS)[:-1])
    for layer in range(DEPTH):
        h = rmsnorm(x, norm1_g[layer])
        proj = h @ w_in[layer]
        q_raw, ff_raw, fb_raw, i_raw, og_raw, u_raw, v_raw, ga, gb = jnp.split(proj, offs, axis=-1)
        ya = hgrn2_branch(q_raw, ff_raw, fb_raw, i_raw, og_raw,
                          layer_lower_bound(lb_fwd, layer), layer_lower_bound(lb_bwd, layer),
                          gn_a[layer]) @ w_oa[layer]
        yb = sgu_branch(u_raw, v_raw, w_sp[layer], b_sp[layer], vnorm_g[layer]) @ w_ob[layer]
        merged = jax.nn.sigmoid(ga) * ya + jax.nn.sigmoid(gb) * yb
        x = x + merged @ w_out[layer]
        x = x + peer_layer(rmsnorm(x, norm2_g[layer]), w_pq[layer], peer_keys[layer],
                           peer_u[layer], peer_v[layer])
    return rmsnorm(x, norm_f)


def setup_inputs(seed: int = 0) -> dict:
    key = jax.random.key(seed)
    ks = jax.random.split(key, 20)
    f32 = jnp.float32
    nrm = lambda k, shape, s: jax.random.normal(k, shape, f32) * s
    return {
        "x_prompt": nrm(ks[0], (BATCH, SEQ, D_MODEL), 1.0),
        "x_sample": nrm(ks[1], (DEC_BATCH, DEC_SEQ, D_MODEL), 1.0),
        "norm1_g": 1.0 + nrm(ks[2], (DEPTH, D_MODEL), 0.02),
        "w_in": nrm(ks[3], (DEPTH, D_MODEL, IN_COLS), D_MODEL ** -0.5),
        "lb_fwd": nrm(ks[4], (DEPTH + 1, A_FW), 0.1),
        "lb_bwd": nrm(ks[5], (DEPTH + 1, A_FW), 0.1),
        "gn_a": 1.0 + nrm(ks[6], (DEPTH, A_DV), 0.02),
        "vnorm_g": 1.0 + nrm(ks[7], (DEPTH, B_W), 0.02),
        "w_sp": nrm(ks[8], (DEPTH, B_GROUPS, B_CHUNK, B_CHUNK), B_CHUNK ** -0.5),
        "b_sp": 1.0 + nrm(ks[9], (DEPTH, B_GROUPS, B_CHUNK), 0.01),
        "w_oa": nrm(ks[10], (DEPTH, A_VW, D_MODEL), A_VW ** -0.5),
        "w_ob": nrm(ks[11], (DEPTH, B_W, D_MODEL), B_W ** -0.5),
        "w_out": nrm(ks[12], (DEPTH, D_MODEL, D_MODEL), D_MODEL ** -0.5),
        "norm2_g": 1.0 + nrm(ks[13], (DEPTH, D_MODEL), 0.02),
        "w_pq": nrm(ks[14], (DEPTH, D_MODEL, P_HEADS * 2 * P_DKEY), D_MODEL ** -0.5),
        "peer_keys": nrm(ks[15], (DEPTH, P_HEADS, 2, P_NKEYS, P_DKEY), P_DKEY ** -0.5),
        "peer_u": nrm(ks[16], (DEPTH, P_N, D_MODEL), D_MODEL ** -0.5),
        "peer_v": nrm(ks[17], (DEPTH, P_N, D_MODEL), 0.3),
        "norm_f": 1.0 + nrm(ks[18], (D_MODEL,), 0.02),
    }


def reference(x_prompt, x_sample, norm1_g, w_in, lb_fwd, lb_bwd, gn_a, vnorm_g, w_sp, b_sp,
              w_oa, w_ob, w_out, norm2_g, w_pq, peer_keys, peer_u, peer_v, norm_f):
    y_prompt = run_trunk(x_prompt, norm1_g, w_in, lb_fwd, lb_bwd, gn_a, vnorm_g, w_sp, b_sp,
                         w_oa, w_ob, w_out, norm2_g, w_pq, peer_keys, peer_u, peer_v, norm_f)
    y_sample = run_trunk(x_sample, norm1_g, w_in, lb_fwd, lb_bwd, gn_a, vnorm_g, w_sp, b_sp,
                         w_oa, w_ob, w_out, norm2_g, w_pq, peer_keys, peer_u, peer_v, norm_f)
    return (y_prompt, y_sample)
```

```python
import functools

import jax
import jax.numpy as jnp
from jax import lax
from jax.experimental import pallas as pl
from jax.experimental.pallas import tpu as pltpu

F32 = jnp.float32
BF16 = jnp.bfloat16

D_MODEL = 1024
A_HEADS = 4
A_DK = 128
A_DV = 128
A_W = A_HEADS * A_DK
A_CHUNK = 64
B_GROUPS = 4
B_GC = 128
B_W = B_GROUPS * B_GC
B_CHUNK = 128
P_HEADS = 8
P_NKEYS = 128
P_DKEY = 128
P_TOPK = 16
P_N = P_NKEYS * P_NKEYS
EPS = 1e-6
IN_COLS = 3 * A_W + 2 * A_W + 2 * B_W + 2 * D_MODEL
COL_Q, COL_FF, COL_FB, COL_I, COL_OG, COL_U, COL_V, COL_GA, COL_GB = 0, 4, 8, 12, 16, 20, 24, 28, 36

VMEM_LIMIT = 56 * 1024 * 1024

NT_DIMS = (((1,), (1,)), ((), ()))
TN_DIMS = (((0,), (0,)), ((), ()))


def _gelu_tanh(x):
    c = 0.7978845608028654
    return x * (0.5 * (1.0 + jnp.tanh(c * (x + 0.044715 * (x * x * x)))))


def _rms(x, g):
    return x * lax.rsqrt(jnp.mean(x * x, axis=-1, keepdims=True) + EPS) * g


IN_TM = 512
IN_CB = 1408


def _inproj_body(x_ref, g_ref, w_ref, o_ref):
    h = _rms(x_ref[...], g_ref[...])
    o_ref[...] = jnp.dot(h.astype(BF16), w_ref[...], preferred_element_type=F32)


def _inproj(x2d, g1, w_in):
    n = x2d.shape[0]
    return pl.pallas_call(
        _inproj_body,
        grid=(IN_COLS // IN_CB, n // IN_TM),
        in_specs=[
            pl.BlockSpec((IN_TM, D_MODEL), lambda c, i: (i, 0)),
            pl.BlockSpec((1, D_MODEL), lambda c, i: (0, 0)),
            pl.BlockSpec((D_MODEL, IN_CB), lambda c, i: (0, c)),
        ],
        out_specs=pl.BlockSpec((IN_TM, IN_CB), lambda c, i: (i, c)),
        out_shape=jax.ShapeDtypeStruct((n, IN_COLS), F32),
        compiler_params=pltpu.CompilerParams(
            dimension_semantics=("arbitrary", "arbitrary"), vmem_limit_bytes=VMEM_LIMIT),
        name="inproj",
    )(x2d, g1, w_in)


SCAN_LT = 512


def _first_softmax_row(lb_ref):
    lb = lb_ref[...]
    m = jnp.max(lb, axis=0, keepdims=True)
    e = jnp.exp(lb - m)
    return e[0:1, :] / jnp.sum(e, axis=0, keepdims=True)


def _scan_body(lbf_ref, lbb_ref, qf_ref, ff_ref, vf_ref, qb_ref, fb_ref, vb_ref,
               of_ref, ob_ref, sf_ref, sb_ref):
    @pl.when(pl.program_id(2) == 0)
    def _():
        sf_ref[...] = jnp.zeros_like(sf_ref)
        sb_ref[...] = jnp.zeros_like(sb_ref)

    lbf = _first_softmax_row(lbf_ref)
    lbb = _first_softmax_row(lbb_ref)
    row = lax.broadcasted_iota(jnp.int32, (A_CHUNK, A_CHUNK), 0)
    col = lax.broadcasted_iota(jnp.int32, (A_CHUNK, A_CHUNK), 1)
    lower = col <= row
    upper = col >= row
    n_chunks = SCAN_LT // A_CHUNK

    def chunk(q_ref, f_ref, v_ref, o_ref, s_ref, lb, mask, c, last_row):
        sl = pl.ds(pl.multiple_of(c * A_CHUNK, A_CHUNK), A_CHUNK)
        qr = q_ref[sl, :]
        q = qr * jax.nn.sigmoid(qr)
        f = lb + (1.0 - lb) * jax.nn.sigmoid(f_ref[sl, :])
        logf = jnp.log(f)
        k = 1.0 - f
        v = v_ref[sl, :].astype(BF16)
        g = jnp.dot(mask.astype(F32), logf, precision=lax.Precision.HIGHEST,
                    preferred_element_type=F32)
        g_last = g[last_row:last_row + 1, :]
        q_dec = (q * jnp.exp(g)).astype(BF16)
        k_inv = (k * jnp.exp(-g)).astype(BF16)
        k_end = (k * jnp.exp(g_last - g)).astype(BF16)
        att = lax.dot_general(q_dec, k_inv, NT_DIMS, preferred_element_type=F32)
        att = jnp.where(mask, att, 0.0).astype(BF16)
        s_t = s_ref[...]
        o = jnp.dot(att, v, preferred_element_type=F32)
        o = o + lax.dot_general(q_dec, s_t.astype(BF16), NT_DIMS, preferred_element_type=F32)
        o_ref[sl, :] = o
        ds_t = lax.dot_general(v, k_end, TN_DIMS, preferred_element_type=F32)
        s_ref[...] = s_t * jnp.exp(g_last) + ds_t

    def step(c, carry):
        chunk(qf_ref, ff_ref, vf_ref, of_ref, sf_ref, lbf, lower, c, A_CHUNK - 1)
        chunk(qb_ref, fb_ref, vb_ref, ob_ref, sb_ref, lbb, upper, n_chunks - 1 - c, 0)
        return carry

    lax.fori_loop(0, n_chunks, step, 0)


def _scan(proj3, lb_fwd, lb_bwd):
    b, l, _ = proj3.shape
    nt = l // SCAN_LT
    blk = (None, SCAN_LT, A_DK)

    def fwd_spec(col0):
        return pl.BlockSpec(blk, lambda bi, h, j: (bi, j, col0 + h))

    def bwd_spec(col0):
        return pl.BlockSpec(blk, lambda bi, h, j: (bi, nt - 1 - j, col0 + h))

    lb_spec = pl.BlockSpec((lb_fwd.shape[0], A_DK), lambda bi, h, j: (0, h))
    out_shape = jax.ShapeDtypeStruct((b, l, A_W), F32)
    return pl.pallas_call(
        _scan_body,
        grid=(b, A_HEADS, nt),
        in_specs=[lb_spec, lb_spec,
                  fwd_spec(COL_Q), fwd_spec(COL_FF), fwd_spec(COL_I),
                  bwd_spec(COL_Q), bwd_spec(COL_FB), bwd_spec(COL_I)],
        out_specs=[pl.BlockSpec(blk, lambda bi, h, j: (bi, j, h)),
                   pl.BlockSpec(blk, lambda bi, h, j: (bi, nt - 1 - j, h))],
        out_shape=[out_shape, out_shape],
        scratch_shapes=[pltpu.VMEM((A_DV, A_DK), F32), pltpu.VMEM((A_DV, A_DK), F32)],
        compiler_params=pltpu.CompilerParams(
            dimension_semantics=("arbitrary", "arbitrary", "arbitrary"),
            vmem_limit_bytes=VMEM_LIMIT),
        name="gla_scan",
    )(lb_fwd, lb_bwd, proj3, proj3, proj3, proj3, proj3, proj3)


MIX_TM = 256


def _mix_body(x_ref, of_ref, ob_ref, og_ref, u_ref, v_ref, ga0_ref, ga1_ref, gb0_ref, gb1_ref,
              gn_ref, vg_ref, wsp_ref, bsp_ref, woa_ref, wob_ref, wout_ref, g2_ref, wpq_ref,
              keys_ref, x2_ref, xnt_ref, sct_ref):
    o = of_ref[...] + ob_ref[...]
    og = og_ref[...]
    gate_a = og * jax.nn.sigmoid(og)
    heads = []
    for h in range(A_HEADS):
        oh = o[:, h * A_DV:(h + 1) * A_DV]
        heads.append(_rms(oh, gn_ref[...]))
    ya_in = jnp.concatenate(heads, axis=-1) * gate_a
    ya = jnp.dot(ya_in.astype(BF16), woa_ref[...], preferred_element_type=F32)

    u = _gelu_tanh(u_ref[...])
    vv = _rms(_gelu_tanh(v_ref[...]), vg_ref[...]).astype(BF16)
    rows = []
    for c in range(MIX_TM // B_CHUNK):
        groups = []
        for g in range(B_GROUPS):
            vg_blk = vv[c * B_CHUNK:(c + 1) * B_CHUNK, g * B_GC:(g + 1) * B_GC]
            m = jnp.dot(wsp_ref[g], vg_blk, preferred_element_type=F32)
            groups.append(m + bsp_ref[:, g:g + 1])
        rows.append(jnp.concatenate(groups, axis=-1))
    mixed = jnp.concatenate(rows, axis=0)
    yb = jnp.dot((u * mixed).astype(BF16), wob_ref[...], preferred_element_type=F32)

    ga = jnp.concatenate([ga0_ref[...], ga1_ref[...]], axis=-1)
    gb = jnp.concatenate([gb0_ref[...], gb1_ref[...]], axis=-1)
    merged = jax.nn.sigmoid(ga) * ya + jax.nn.sigmoid(gb) * yb
    x2 = x_ref[...] + jnp.dot(merged.astype(BF16), wout_ref[...], preferred_element_type=F32)
    x2_ref[...] = x2

    xn = _rms(x2, g2_ref[...])
    xn_bf = xn.astype(BF16)
    xnt_ref[...] = xn.T.astype(BF16)
    qh = jnp.dot(xn_bf, wpq_ref[...], preferred_element_type=F32).astype(BF16)
    for hp in range(2 * P_HEADS):
        q_hp = qh[:, hp * P_DKEY:(hp + 1) * P_DKEY]
        sct_ref[hp] = lax.dot_general(keys_ref[hp], q_hp, NT_DIMS, preferred_element_type=F32)


def _mix(x2d, proj, o_f, o_b, gn, vg, w_sp, b_sp_t, w_oa, w_ob, w_out, g2, w_pq, keys):
    n = x2d.shape[0]
    tm = MIX_TM

    def col_spec(col0):
        return pl.BlockSpec((tm, 512), lambda i: (i, col0 // 4))

    def full(a):
        return pl.BlockSpec(a.shape, lambda i: (0,) * a.ndim)

    return pl.pallas_call(
        _mix_body,
        grid=(n // tm,),
        in_specs=[
            pl.BlockSpec((tm, D_MODEL), lambda i: (i, 0)),
            pl.BlockSpec((tm, A_W), lambda i: (i, 0)),
            pl.BlockSpec((tm, A_W), lambda i: (i, 0)),
            col_spec(COL_OG), col_spec(COL_U), col_spec(COL_V),
            col_spec(COL_GA), col_spec(COL_GA + 4), col_spec(COL_GB), col_spec(COL_GB + 4),
            full(gn), full(vg), full(w_sp), full(b_sp_t), full(w_oa), full(w_ob), full(w_out),
            full(g2), full(w_pq), full(keys),
        ],
        out_specs=[
            pl.BlockSpec((tm, D_MODEL), lambda i: (i, 0)),
            pl.BlockSpec((D_MODEL, tm), lambda i: (0, i)),
            pl.BlockSpec((2 * P_HEADS, P_NKEYS, tm), lambda i: (0, 0, i)),
        ],
        out_shape=[
            jax.ShapeDtypeStruct((n, D_MODEL), F32),
            jax.ShapeDtypeStruct((D_MODEL, n), BF16),
            jax.ShapeDtypeStruct((2 * P_HEADS, P_NKEYS, n), F32),
        ],
        compiler_params=pltpu.CompilerParams(
            dimension_semantics=("arbitrary",), vmem_limit_bytes=VMEM_LIMIT),
        name="mix",
    )(x2d, o_f, o_b, proj, proj, proj, proj, proj, proj, proj,
      gn, vg, w_sp, b_sp_t, w_oa, w_ob, w_out, g2, w_pq, keys)


SEL_TL = 128
NEG_INF = float("-inf")


def _top_values(s, vals_ref):
    work = s
    for i in range(P_TOPK):
        m = jnp.max(work, axis=0, keepdims=True)
        vals_ref[i:i + 1, :] = m
        work = jnp.where(work == m, NEG_INF, work)


def _select_body(sct_ref, c_ref, u_ref, v_ref, v0_ref, v1_ref):
    for h in range(P_HEADS):
        s0 = sct_ref[2 * h]
        s1 = sct_ref[2 * h + 1]
        _top_values(s0, v0_ref)
        _top_values(s1, v1_ref)
        top0 = v0_ref[...]
        top1 = v1_ref[...]
        m0 = top0[0:1, :]
        m1 = top1[0:1, :]
        cands = [m0 + top1]
        for i in range(1, P_TOPK):
            cands.append(top0[i:i + 1, :] + top1[0:8, :])
        best = m0 + m1
        z = jnp.zeros_like(best)
        tau = best
        for _ in range(P_TOPK):
            m = functools.reduce(jnp.maximum, [jnp.max(cd, axis=0, keepdims=True) for cd in cands])
            z = z + jnp.exp(m - best)
            tau = m
            cands = [jnp.where(cd == m, NEG_INF, cd) for cd in cands]
        c_ref[h] = tau - s0
        u_ref[h] = jnp.exp(s0 - m0) / z
        v_ref[h] = jnp.exp(s1 - m1)


def _select(sct):
    n = sct.shape[-1]
    tl = SEL_TL
    out_shape = jax.ShapeDtypeStruct((P_HEADS, P_NKEYS, n), F32)
    out_spec = pl.BlockSpec((P_HEADS, P_NKEYS, tl), lambda i: (0, 0, i))
    return pl.pallas_call(
        _select_body,
        grid=(n // tl,),
        in_specs=[pl.BlockSpec((2 * P_HEADS, P_NKEYS, tl), lambda i: (0, 0, i))],
        out_specs=[out_spec, out_spec, out_spec],
        out_shape=[out_shape, out_shape, out_shape],
        scratch_shapes=[pltpu.VMEM((P_TOPK, tl), F32), pltpu.VMEM((P_TOPK, tl), F32)],
        compiler_params=pltpu.CompilerParams(
            dimension_semantics=("arbitrary",), vmem_limit_bytes=VMEM_LIMIT),
        name="select",
    )(sct)


PEER_T = 256
PEER_EB = 512


def _peer_body(xnt_ref, s1_ref, c_ref, u_ref, v_ref, pu_ref, pvt_ref, x2_ref, gf_ref,
               y_ref, acc_ref, w_ref):
    i = pl.program_id(1)

    @pl.when(i == 0)
    def _():
        acc_ref[...] = jnp.zeros_like(acc_ref)

    act = _gelu_tanh(jnp.dot(pu_ref[...], xnt_ref[...], preferred_element_type=F32))
    n_a = PEER_EB // P_NKEYS
    for al in range(n_a):
        a = i * n_a + al
        g = jnp.zeros((P_NKEYS, PEER_T), F32)
        for h in range(P_HEADS):
            c = c_ref[h, pl.ds(a, 1), :]
            u = u_ref[h, pl.ds(a, 1), :]
            g = g + jnp.where(s1_ref[h] >= c, v_ref[h] * u, 0.0)
        w_ref[al * P_NKEYS:(al + 1) * P_NKEYS, :] = (
            g * act[al * P_NKEYS:(al + 1) * P_NKEYS, :]).astype(BF16)
    acc_ref[...] += jnp.dot(pvt_ref[...], w_ref[...], preferred_element_type=F32)

    @pl.when(i == pl.num_programs(1) - 1)
    def _():
        y_ref[...] = _rms(x2_ref[...] + acc_ref[...].T, gf_ref[...])


def _peer(xnt, sct4, c, u, v, pu, pvt, x2, gf):
    n = x2.shape[0]
    t, eb = PEER_T, PEER_EB
    sel_spec = pl.BlockSpec((P_HEADS, P_NKEYS, t), lambda j, i: (0, 0, j))
    return pl.pallas_call(
        _peer_body,
        grid=(n // t, P_N // eb),
        in_specs=[
            pl.BlockSpec((D_MODEL, t), lambda j, i: (0, j)),
            pl.BlockSpec((P_HEADS, None, P_NKEYS, t), lambda j, i: (0, 1, 0, j)),
            sel_spec, sel_spec, sel_spec,
            pl.BlockSpec((eb, D_MODEL), lambda j, i: (i, 0)),
            pl.BlockSpec((D_MODEL, eb), lambda j, i: (0, i)),
            pl.BlockSpec((t, D_MODEL), lambda j, i: (j, 0)),
            pl.BlockSpec((1, D_MODEL), lambda j, i: (0, 0)),
        ],
        out_specs=pl.BlockSpec((t, D_MODEL), lambda j, i: (j, 0)),
        out_shape=jax.ShapeDtypeStruct((n, D_MODEL), F32),
        scratch_shapes=[pltpu.VMEM((D_MODEL, t), F32), pltpu.VMEM((eb, t), BF16)],
        compiler_params=pltpu.CompilerParams(
            dimension_semantics=("arbitrary", "arbitrary"), vmem_limit_bytes=VMEM_LIMIT),
        name="peer",
    )(xnt, sct4, c, u, v, pu, pvt, x2, gf)


def _trunk(x, p):
    b, l, d = x.shape
    n = b * l
    x2d = x.reshape(n, d)
    proj = _inproj(x2d, p["g1"], p["w_in"])
    o_f, o_b = _scan(proj.reshape(b, l, IN_COLS), p["lb_fwd"], p["lb_bwd"])
    x2, xnt, sct = _mix(x2d, proj, o_f.reshape(n, A_W), o_b.reshape(n, A_W), p["gn"], p["vg"],
                        p["w_sp"], p["b_sp_t"], p["w_oa"], p["w_ob"], p["w_out"], p["g2"],
                        p["w_pq"], p["keys"])
    c, u, v = _select(sct)
    sct4 = sct.reshape(P_HEADS, 2, P_NKEYS, n)
    y = _peer(xnt, sct4, c, u, v, p["pu"], p["pvt"], x2, p["gf"])
    return y.reshape(b, l, d)


def kernel(x_prompt, x_sample, norm1_g, w_in, lb_fwd, lb_bwd, gn_a, vnorm_g, w_sp, b_sp, w_oa, w_ob,
           w_out, norm2_g, w_pq, peer_keys, peer_u, peer_v, norm_f):
    layer = 0
    p = {
        "g1": norm1_g[layer].reshape(1, D_MODEL),
        "w_in": w_in[layer].astype(BF16),
        "lb_fwd": lb_fwd,
        "lb_bwd": lb_bwd,
        "gn": gn_a[layer].reshape(1, A_DV),
        "vg": vnorm_g[layer].reshape(1, B_W),
        "w_sp": w_sp[layer].astype(BF16),
        "b_sp_t": b_sp[layer].T,
        "w_oa": w_oa[layer].astype(BF16),
        "w_ob": w_ob[layer].astype(BF16),
        "w_out": w_out[layer].astype(BF16),
        "g2": norm2_g[layer].reshape(1, D_MODEL),
        "w_pq": w_pq[layer].astype(BF16),
        "keys": peer_keys[layer].reshape(2 * P_HEADS, P_NKEYS, P_DKEY).astype(BF16),
        "pu": peer_u[layer].astype(BF16),
        "pvt": peer_v[layer].T.astype(BF16),
        "gf": norm_f.reshape(1, D_MODEL),
    }
    return (_trunk(x_prompt, p), _trunk(x_sample, p))
```

```python
import functools

import jax
import jax.numpy as jnp
from jax import lax
from jax.experimental import pallas as pl
from jax.experimental.pallas import tpu as pltpu

F32 = jnp.float32
BF16 = jnp.bfloat16

D_MODEL = 1024
A_HEADS = 4
A_DK = 128
A_DV = 128
A_W = A_HEADS * A_DK
A_CHUNK = 64
B_GROUPS = 4
B_GC = 128
B_W = B_GROUPS * B_GC
B_CHUNK = 128
P_HEADS = 8
P_NKEYS = 128
P_DKEY = 128
P_TOPK = 16
P_N = P_NKEYS * P_NKEYS
EPS = 1e-6
IN_COLS = 3 * A_W + 2 * A_W + 2 * B_W + 2 * D_MODEL
COL_Q, COL_FF, COL_FB, COL_I, COL_OG, COL_U, COL_V, COL_GA, COL_GB = 0, 4, 8, 12, 16, 20, 24, 28, 36

VMEM_LIMIT = 56 * 1024 * 1024

NT_DIMS = (((1,), (1,)), ((), ()))
TN_DIMS = (((0,), (0,)), ((), ()))


def _gelu_tanh(x):
    c = 0.7978845608028654
    return x * (0.5 * (1.0 + jnp.tanh(c * (x + 0.044715 * (x * x * x)))))


def _rms(x, g):
    return x * lax.rsqrt(jnp.mean(x * x, axis=-1, keepdims=True) + EPS) * g


IN_TM = 512
IN_CB = 1408


def _inproj_body(x_ref, g_ref, w_ref, o_ref):
    h = _rms(x_ref[...], g_ref[...])
    o_ref[...] = jnp.dot(h.astype(BF16), w_ref[...], preferred_element_type=F32)


def _inproj(x2d, g1, w_in):
    n = x2d.shape[0]
    return pl.pallas_call(
        _inproj_body,
        grid=(IN_COLS // IN_CB, n // IN_TM),
        in_specs=[
            pl.BlockSpec((IN_TM, D_MODEL), lambda c, i: (i, 0)),
            pl.BlockSpec((1, D_MODEL), lambda c, i: (0, 0)),
            pl.BlockSpec((D_MODEL, IN_CB), lambda c, i: (0, c)),
        ],
        out_specs=pl.BlockSpec((IN_TM, IN_CB), lambda c, i: (i, c)),
        out_shape=jax.ShapeDtypeStruct((n, IN_COLS), F32),
        compiler_params=pltpu.CompilerParams(
            dimension_semantics=("arbitrary", "arbitrary"), vmem_limit_bytes=VMEM_LIMIT),
        name="inproj",
    )(x2d, g1, w_in)


SCAN_LT = 512


def _first_softmax_row(lb_ref):
    lb = lb_ref[...]
    m = jnp.max(lb, axis=0, keepdims=True)
    e = jnp.exp(lb - m)
    return e[0:1, :] / jnp.sum(e, axis=0, keepdims=True)


def _scan_body(lbf_ref, lbb_ref, qf_ref, ff_ref, vf_ref, qb_ref, fb_ref, vb_ref,
               of_ref, ob_ref, sf_ref, sb_ref):
    @pl.when(pl.program_id(2) == 0)
    def _():
        sf_ref[...] = jnp.zeros_like(sf_ref)
        sb_ref[...] = jnp.zeros_like(sb_ref)

    lbf = _first_softmax_row(lbf_ref)
    lbb = _first_softmax_row(lbb_ref)
    row = lax.broadcasted_iota(jnp.int32, (A_CHUNK, A_CHUNK), 0)
    col = lax.broadcasted_iota(jnp.int32, (A_CHUNK, A_CHUNK), 1)
    lower = col <= row
    upper = col >= row
    n_chunks = SCAN_LT // A_CHUNK

    def chunk(q_ref, f_ref, v_ref, o_ref, s_ref, lb, mask, c, last_row):
        sl = pl.ds(pl.multiple_of(c * A_CHUNK, A_CHUNK), A_CHUNK)
        qr = q_ref[sl, :]
        q = qr * jax.nn.sigmoid(qr)
        f = lb + (1.0 - lb) * jax.nn.sigmoid(f_ref[sl, :])
        logf = jnp.log(f)
        k = 1.0 - f
        v = v_ref[sl, :].astype(BF16)
        g = jnp.dot(mask.astype(F32), logf, precision=lax.Precision.HIGHEST,
                    preferred_element_type=F32)
        g_last = g[last_row:last_row + 1, :]
        q_dec = (q * jnp.exp(g)).astype(BF16)
        k_inv = (k * jnp.exp(-g)).astype(BF16)
        k_end = (k * jnp.exp(g_last - g)).astype(BF16)
        att = lax.dot_general(q_dec, k_inv, NT_DIMS, preferred_element_type=F32)
        att = jnp.where(mask, att, 0.0).astype(BF16)
        s_t = s_ref[...]
        o = jnp.dot(att, v, preferred_element_type=F32)
        o = o + lax.dot_general(q_dec, s_t.astype(BF16), NT_DIMS, preferred_element_type=F32)
        o_ref[sl, :] = o
        ds_t = lax.dot_general(v, k_end, TN_DIMS, preferred_element_type=F32)
        s_ref[...] = s_t * jnp.exp(g_last) + ds_t

    def step(c, carry):
        chunk(qf_ref, ff_ref, vf_ref, of_ref, sf_ref, lbf, lower, c, A_CHUNK - 1)
        chunk(qb_ref, fb_ref, vb_ref, ob_ref, sb_ref, lbb, upper, n_chunks - 1 - c, 0)
        return carry

    lax.fori_loop(0, n_chunks, step, 0)


def _scan(proj3, lb_fwd, lb_bwd):
    b, l, _ = proj3.shape
    nt = l // SCAN_LT
    blk = (None, SCAN_LT, A_DK)

    def fwd_spec(col0):
        return pl.BlockSpec(blk, lambda bi, h, j: (bi, j, col0 + h))

    def bwd_spec(col0):
        return pl.BlockSpec(blk, lambda bi, h, j: (bi, nt - 1 - j, col0 + h))

    lb_spec = pl.BlockSpec((lb_fwd.shape[0], A_DK), lambda bi, h, j: (0, h))
    out_shape = jax.ShapeDtypeStruct((b, l, A_W), F32)
    return pl.pallas_call(
        _scan_body,
        grid=(b, A_HEADS, nt),
        in_specs=[lb_spec, lb_spec,
                  fwd_spec(COL_Q), fwd_spec(COL_FF), fwd_spec(COL_I),
                  bwd_spec(COL_Q), bwd_spec(COL_FB), bwd_spec(COL_I)],
        out_specs=[pl.BlockSpec(blk, lambda bi, h, j: (bi, j, h)),
                   pl.BlockSpec(blk, lambda bi, h, j: (bi, nt - 1 - j, h))],
        out_shape=[out_shape, out_shape],
        scratch_shapes=[pltpu.VMEM((A_DV, A_DK), F32), pltpu.VMEM((A_DV, A_DK), F32)],
        compiler_params=pltpu.CompilerParams(
            dimension_semantics=("arbitrary", "arbitrary", "arbitrary"),
            vmem_limit_bytes=VMEM_LIMIT),
        name="gla_scan",
    )(lb_fwd, lb_bwd, proj3, proj3, proj3, proj3, proj3, proj3)


MIX_TM = 256


def _mix_body(x_ref, of_ref, ob_ref, og_ref, u_ref, v_ref, ga0_ref, ga1_ref, gb0_ref, gb1_ref,
              gn_ref, vg_ref, wsp_ref, bsp_ref, woa_ref, wob_ref, wout_ref, g2_ref, wpq_ref,
              keys_ref, x2_ref, xnt_ref, sct_ref):
    o = of_ref[...] + ob_ref[...]
    og = og_ref[...]
    gate_a = og * jax.nn.sigmoid(og)
    heads = []
    for h in range(A_HEADS):
        oh = o[:, h * A_DV:(h + 1) * A_DV]
        heads.append(_rms(oh, gn_ref[...]))
    ya_in = jnp.concatenate(heads, axis=-1) * gate_a
    ya = jnp.dot(ya_in.astype(BF16), woa_ref[...], preferred_element_type=F32)

    u = _gelu_tanh(u_ref[...])
    vv = _rms(_gelu_tanh(v_ref[...]), vg_ref[...]).astype(BF16)
    rows = []
    for c in range(MIX_TM // B_CHUNK):
        groups = []
        for g in range(B_GROUPS):
            vg_blk = vv[c * B_CHUNK:(c + 1) * B_CHUNK, g * B_GC:(g + 1) * B_GC]
            m = jnp.dot(wsp_ref[g], vg_blk, preferred_element_type=F32)
            groups.append(m + bsp_ref[:, g:g + 1])
        rows.append(jnp.concatenate(groups, axis=-1))
    mixed = jnp.concatenate(rows, axis=0)
    yb = jnp.dot((u * mixed).astype(BF16), wob_ref[...], preferred_element_type=F32)

    ga = jnp.concatenate([ga0_ref[...], ga1_ref[...]], axis=-1)
    gb = jnp.concatenate([gb0_ref[...], gb1_ref[...]], axis=-1)
    merged = jax.nn.sigmoid(ga) * ya + jax.nn.sigmoid(gb) * yb
    x2 = x_ref[...] + jnp.dot(merged.astype(BF16), wout_ref[...], preferred_element_type=F32)
    x2_ref[...] = x2

    xn = _rms(x2, g2_ref[...])
    xn_bf = xn.astype(BF16)
    xnt_ref[...] = xn.T.astype(BF16)
    qh = jnp.dot(xn_bf, wpq_ref[...], preferred_element_type=F32).astype(BF16)
    for hp in range(2 * P_HEADS):
        q_hp = qh[:, hp * P_DKEY:(hp + 1) * P_DKEY]
        sct_ref[hp] = lax.dot_general(keys_ref[hp], q_hp, NT_DIMS, preferred_element_type=F32)


def _mix(x2d, proj, o_f, o_b, gn, vg, w_sp, b_sp_t, w_oa, w_ob, w_out, g2, w_pq, keys):
    n = x2d.shape[0]
    tm = MIX_TM

    def col_spec(col0):
        return pl.BlockSpec((tm, 512), lambda i: (i, col0 // 4))

    def full(a):
        return pl.BlockSpec(a.shape, lambda i: (0,) * a.ndim)

    return pl.pallas_call(
        _mix_body,
        grid=(n // tm,),
        in_specs=[
            pl.BlockSpec((tm, D_MODEL), lambda i: (i, 0)),
            pl.BlockSpec((tm, A_W), lambda i: (i, 0)),
            pl.BlockSpec((tm, A_W), lambda i: (i, 0)),
            col_spec(COL_OG), col_spec(COL_U), col_spec(COL_V),
            col_spec(COL_GA), col_spec(COL_GA + 4), col_spec(COL_GB), col_spec(COL_GB + 4),
            full(gn), full(vg), full(w_sp), full(b_sp_t), full(w_oa), full(w_ob), full(w_out),
            full(g2), full(w_pq), full(keys),
        ],
        out_specs=[
            pl.BlockSpec((tm, D_MODEL), lambda i: (i, 0)),
            pl.BlockSpec((D_MODEL, tm), lambda i: (0, i)),
            pl.BlockSpec((2 * P_HEADS, P_NKEYS, tm), lambda i: (0, 0, i)),
        ],
        out_shape=[
            jax.ShapeDtypeStruct((n, D_MODEL), F32),
            jax.ShapeDtypeStruct((D_MODEL, n), BF16),
            jax.ShapeDtypeStruct((2 * P_HEADS, P_NKEYS, n), F32),
        ],
        compiler_params=pltpu.CompilerParams(
            dimension_semantics=("arbitrary",), vmem_limit_bytes=VMEM_LIMIT),
        name="mix",
    )(x2d, o_f, o_b, proj, proj, proj, proj, proj, proj, proj,
      gn, vg, w_sp, b_sp_t, w_oa, w_ob, w_out, g2, w_pq, keys)


SEL_TL = 128
NEG_INF = float("-inf")
NOT_RANKED = 99.0


def _top_values(s, vals_ref):
    work = s
    rank = jnp.full(s.shape, NOT_RANKED, F32)
    for i in range(P_TOPK):
        m = jnp.max(work, axis=0, keepdims=True)
        vals_ref[i:i + 1, :] = m
        hit = work == m
        rank = jnp.where(hit, float(i + 1), rank)
        work = jnp.where(hit, NEG_INF, work)
    return rank


def _select_body(sct_ref, rank_ref, v_ref, n_ref, u_ref, v0_ref, v1_ref):
    for h in range(P_HEADS):
        s0 = sct_ref[2 * h]
        s1 = sct_ref[2 * h + 1]
        rank0 = _top_values(s0, v0_ref)
        rank1 = _top_values(s1, v1_ref)
        top0 = v0_ref[...]
        top1 = v1_ref[...]
        m0 = top0[0:1, :]
        m1 = top1[0:1, :]
        cands = [m0 + top1]
        for i in range(1, P_TOPK):
            cands.append(top0[i:i + 1, :] + top1[0:8, :])
        best = m0 + m1
        z = jnp.zeros_like(best)
        tau = best
        for _ in range(P_TOPK):
            m = functools.reduce(jnp.maximum, [jnp.max(cd, axis=0, keepdims=True) for cd in cands])
            z = z + jnp.exp(m - best)
            tau = m
            cands = [jnp.where(cd == m, NEG_INF, cd) for cd in cands]
        n = jnp.zeros_like(s0)
        for i in range(P_TOPK):
            cnt = jnp.sum(jnp.where(top0[i:i + 1, :] + top1 >= tau, 1.0, 0.0), axis=0, keepdims=True)
            n = jnp.where(rank0 == float(i + 1), cnt, n)
        rank_ref[h] = rank1.astype(BF16)
        v_ref[h] = jnp.exp(s1 - m1).astype(BF16)
        n_ref[h] = n
        u_ref[h] = jnp.exp(s0 - m0) / z


def _select(sct):
    n = sct.shape[-1]
    tl = SEL_TL
    spec = pl.BlockSpec((P_HEADS, P_NKEYS, tl), lambda i: (0, 0, i))
    shape16 = jax.ShapeDtypeStruct((P_HEADS, P_NKEYS, n), BF16)
    shape32 = jax.ShapeDtypeStruct((P_HEADS, P_NKEYS, n), F32)
    return pl.pallas_call(
        _select_body,
        grid=(n // tl,),
        in_specs=[pl.BlockSpec((2 * P_HEADS, P_NKEYS, tl), lambda i: (0, 0, i))],
        out_specs=[spec, spec, spec, spec],
        out_shape=[shape16, shape16, shape32, shape32],
        scratch_shapes=[pltpu.VMEM((P_TOPK, tl), F32), pltpu.VMEM((P_TOPK, tl), F32)],
        compiler_params=pltpu.CompilerParams(
            dimension_semantics=("arbitrary",), vmem_limit_bytes=VMEM_LIMIT),
        name="select",
    )(sct)


PEER_T = 256
PEER_EB = 512
PEER_NI = P_N // PEER_EB
BF16_ROWS = 16


def _peer_body(xnt_ref, rank_ref, v_ref, n_ref, u_ref, pu_ref, pvt_ref, x2_ref, gf_ref,
               y_ref, acc_ref, act_ref, w_ref, *, n_blocks):
    s = pl.program_id(0)
    p2 = jnp.clip(s - 1, 0, n_blocks - 1)
    p3 = jnp.clip(s - 2, 0, n_blocks - 1)
    i2 = p2 % PEER_NI
    i3 = p3 % PEER_NI
    cur = s % 2
    prev = 1 - cur

    @pl.when(s == 0)
    def _():
        act_ref[...] = jnp.zeros_like(act_ref)
        w_ref[...] = jnp.zeros_like(w_ref)

    @pl.when(i3 == 0)
    def _():
        acc_ref[...] = jnp.zeros_like(acc_ref)

    acc_ref[...] += jnp.dot(pvt_ref[...], w_ref[prev], preferred_element_type=F32)

    groups = P_NKEYS // BF16_ROWS
    for al in range(PEER_EB // P_NKEYS):
        a = i2 * (PEER_EB // P_NKEYS) + al
        g = jnp.zeros((groups, BF16_ROWS, PEER_T), BF16)
        for h in range(P_HEADS):
            nb = jnp.broadcast_to(n_ref[h, pl.ds(a, 1), :], (BF16_ROWS, PEER_T)).astype(BF16)
            ub = jnp.broadcast_to(u_ref[h, pl.ds(a, 1), :], (BF16_ROWS, PEER_T)).astype(BF16)
            r = rank_ref[h].reshape(groups, BF16_ROWS, PEER_T)
            vv = v_ref[h].reshape(groups, BF16_ROWS, PEER_T)
            g = g + jnp.where(r <= nb[None], vv, jnp.zeros_like(vv)) * ub[None]
        rows = slice(al * P_NKEYS, (al + 1) * P_NKEYS)
        act = act_ref[prev, rows, :].astype(F32)
        w_ref[cur, rows, :] = g.reshape(P_NKEYS, PEER_T) * _gelu_tanh(act).astype(BF16)

    act_ref[cur] = jnp.dot(pu_ref[...], xnt_ref[...], preferred_element_type=F32).astype(BF16)

    @pl.when(jnp.logical_and(s >= 2, i3 == PEER_NI - 1))
    def _():
        y_ref[...] = _rms(x2_ref[...] + acc_ref[...].T, gf_ref[...])


def _peer(xnt, rank1, v, nn, u, pu, pvt, x2, gf):
    n = x2.shape[0]
    t, eb = PEER_T, PEER_EB
    n_blocks = (n // t) * PEER_NI
    last = n_blocks - 1

    def pair(s, lag):
        return jnp.clip(s - lag, 0, last)

    def sel_spec():
        return pl.BlockSpec((P_HEADS, P_NKEYS, t), lambda s: (0, 0, pair(s, 1) // PEER_NI))

    return pl.pallas_call(
        functools.partial(_peer_body, n_blocks=n_blocks),
        grid=(n_blocks + 2,),
        in_specs=[
            pl.BlockSpec((D_MODEL, t), lambda s: (0, pair(s, 0) // PEER_NI)),
            sel_spec(), sel_spec(), sel_spec(), sel_spec(),
            pl.BlockSpec((eb, D_MODEL), lambda s: (pair(s, 0) % PEER_NI, 0)),
            pl.BlockSpec((D_MODEL, eb), lambda s: (0, pair(s, 2) % PEER_NI)),
            pl.BlockSpec((t, D_MODEL), lambda s: (pair(s, 2) // PEER_NI, 0)),
            pl.BlockSpec((1, D_MODEL), lambda s: (0, 0)),
        ],
        out_specs=pl.BlockSpec((t, D_MODEL), lambda s: (pair(s, 2) // PEER_NI, 0)),
        out_shape=jax.ShapeDtypeStruct((n, D_MODEL), F32),
        scratch_shapes=[pltpu.VMEM((D_MODEL, t), F32),
                        pltpu.VMEM((2, eb, t), BF16),
                        pltpu.VMEM((2, eb, t), BF16)],
        compiler_params=pltpu.CompilerParams(
            dimension_semantics=("arbitrary",), vmem_limit_bytes=VMEM_LIMIT),
        name="peer",
    )(xnt, rank1, v, nn, u, pu, pvt, x2, gf)


def _trunk(x, p):
    b, l, d = x.shape
    n = b * l
    x2d = x.reshape(n, d)
    proj = _inproj(x2d, p["g1"], p["w_in"])
    o_f, o_b = _scan(proj.reshape(b, l, IN_COLS), p["lb_fwd"], p["lb_bwd"])
    x2, xnt, sct = _mix(x2d, proj, o_f.reshape(n, A_W), o_b.reshape(n, A_W), p["gn"], p["vg"],
                        p["w_sp"], p["b_sp_t"], p["w_oa"], p["w_ob"], p["w_out"], p["g2"],
                        p["w_pq"], p["keys"])
    rank1, v, nn, u = _select(sct)
    y = _peer(xnt, rank1, v, nn, u, p["pu"], p["pvt"], x2, p["gf"])
    return y.reshape(b, l, d)


def kernel(x_prompt, x_sample, norm1_g, w_in, lb_fwd, lb_bwd, gn_a, vnorm_g, w_sp, b_sp, w_oa, w_ob,
           w_out, norm2_g, w_pq, peer_keys, peer_u, peer_v, norm_f):
    layer = 0
    p = {
        "g1": norm1_g[layer].reshape(1, D_MODEL),
        "w_in": w_in[layer].astype(BF16),
        "lb_fwd": lb_fwd,
        "lb_bwd": lb_bwd,
        "gn": gn_a[layer].reshape(1, A_DV),
        "vg": vnorm_g[layer].reshape(1, B_W),
        "w_sp": w_sp[layer].astype(BF16),
        "b_sp_t": b_sp[layer].T,
        "w_oa": w_oa[layer].astype(BF16),
        "w_ob": w_ob[layer].astype(BF16),
        "w_out": w_out[layer].astype(BF16),
        "g2": norm2_g[layer].reshape(1, D_MODEL),
        "w_pq": w_pq[layer].astype(BF16),
        "keys": peer_keys[layer].reshape(2 * P_HEADS, P_NKEYS, P_DKEY).astype(BF16),
        "pu": peer_u[layer].astype(BF16),
        "pvt": peer_v[layer].T.astype(BF16),
        "gf": norm_f.reshape(1, D_MODEL),
    }
    return (_trunk(x_prompt, p), _trunk(x_sample, p))
```

```python
import functools

import jax
import jax.numpy as jnp
from jax import lax
from jax.experimental import pallas as pl
from jax.experimental.pallas import tpu as pltpu

F32 = jnp.float32
BF16 = jnp.bfloat16

D_MODEL = 1024
A_HEADS = 4
A_DK = 128
A_DV = 128
A_W = A_HEADS * A_DK
A_CHUNK = 64
B_GROUPS = 4
B_GC = 128
B_W = B_GROUPS * B_GC
B_CHUNK = 128
P_HEADS = 8
P_NKEYS = 128
P_DKEY = 128
P_TOPK = 16
P_N = P_NKEYS * P_NKEYS
EPS = 1e-6
IN_COLS = 3 * A_W + 2 * A_W + 2 * B_W + 2 * D_MODEL
COL_Q, COL_FF, COL_FB, COL_I, COL_OG, COL_U, COL_V, COL_GA, COL_GB = 0, 4, 8, 12, 16, 20, 24, 28, 36

VMEM_LIMIT = 56 * 1024 * 1024

NT_DIMS = (((1,), (1,)), ((), ()))
TN_DIMS = (((0,), (0,)), ((), ()))


def _gelu_tanh(x):
    c = 0.7978845608028654
    return x * (0.5 * (1.0 + jnp.tanh(c * (x + 0.044715 * (x * x * x)))))


def _rms(x, g):
    return x * lax.rsqrt(jnp.mean(x * x, axis=-1, keepdims=True) + EPS) * g


IN_TM = 512
IN_CB = 1408


def _inproj_body(x_ref, g_ref, w_ref, o_ref):
    h = _rms(x_ref[...], g_ref[...])
    o_ref[...] = jnp.dot(h.astype(BF16), w_ref[...], preferred_element_type=F32)


def _inproj(x2d, g1, w_in):
    n = x2d.shape[0]
    return pl.pallas_call(
        _inproj_body,
        grid=(IN_COLS // IN_CB, n // IN_TM),
        in_specs=[
            pl.BlockSpec((IN_TM, D_MODEL), lambda c, i: (i, 0)),
            pl.BlockSpec((1, D_MODEL), lambda c, i: (0, 0)),
            pl.BlockSpec((D_MODEL, IN_CB), lambda c, i: (0, c)),
        ],
        out_specs=pl.BlockSpec((IN_TM, IN_CB), lambda c, i: (i, c)),
        out_shape=jax.ShapeDtypeStruct((n, IN_COLS), F32),
        compiler_params=pltpu.CompilerParams(
            dimension_semantics=("arbitrary", "arbitrary"), vmem_limit_bytes=VMEM_LIMIT),
        name="inproj",
    )(x2d, g1, w_in)


SCAN_LT = 512


def _first_softmax_row(lb_ref):
    lb = lb_ref[...]
    m = jnp.max(lb, axis=0, keepdims=True)
    e = jnp.exp(lb - m)
    return e[0:1, :] / jnp.sum(e, axis=0, keepdims=True)


def _scan_body(lbf_ref, lbb_ref, qf_ref, ff_ref, vf_ref, qb_ref, fb_ref, vb_ref,
               of_ref, ob_ref, sf_ref, sb_ref):
    @pl.when(pl.program_id(2) == 0)
    def _():
        sf_ref[...] = jnp.zeros_like(sf_ref)
        sb_ref[...] = jnp.zeros_like(sb_ref)

    lbf = _first_softmax_row(lbf_ref)
    lbb = _first_softmax_row(lbb_ref)
    row = lax.broadcasted_iota(jnp.int32, (A_CHUNK, A_CHUNK), 0)
    col = lax.broadcasted_iota(jnp.int32, (A_CHUNK, A_CHUNK), 1)
    lower = col <= row
    upper = col >= row
    n_chunks = SCAN_LT // A_CHUNK

    def chunk(q_ref, f_ref, v_ref, o_ref, s_ref, lb, mask, c, last_row):
        sl = pl.ds(pl.multiple_of(c * A_CHUNK, A_CHUNK), A_CHUNK)
        qr = q_ref[sl, :]
        q = qr * jax.nn.sigmoid(qr)
        f = lb + (1.0 - lb) * jax.nn.sigmoid(f_ref[sl, :])
        logf = jnp.log(f)
        k = 1.0 - f
        v = v_ref[sl, :].astype(BF16)
        g = jnp.dot(mask.astype(F32), logf, precision=lax.Precision.HIGHEST,
                    preferred_element_type=F32)
        g_last = g[last_row:last_row + 1, :]
        q_dec = (q * jnp.exp(g)).astype(BF16)
        k_inv = (k * jnp.exp(-g)).astype(BF16)
        k_end = (k * jnp.exp(g_last - g)).astype(BF16)
        att = lax.dot_general(q_dec, k_inv, NT_DIMS, preferred_element_type=F32)
        att = jnp.where(mask, att, 0.0).astype(BF16)
        s_t = s_ref[...]
        o = jnp.dot(att, v, preferred_element_type=F32)
        o = o + lax.dot_general(q_dec, s_t.astype(BF16), NT_DIMS, preferred_element_type=F32)
        o_ref[sl, :] = o
        ds_t = lax.dot_general(v, k_end, TN_DIMS, preferred_element_type=F32)
        s_ref[...] = s_t * jnp.exp(g_last) + ds_t

    def step(c, carry):
        chunk(qf_ref, ff_ref, vf_ref, of_ref, sf_ref, lbf, lower, c, A_CHUNK - 1)
        chunk(qb_ref, fb_ref, vb_ref, ob_ref, sb_ref, lbb, upper, n_chunks - 1 - c, 0)
        return carry

    lax.fori_loop(0, n_chunks, step, 0)


def _scan(proj3, lb_fwd, lb_bwd):
    b, l, _ = proj3.shape
    nt = l // SCAN_LT
    blk = (None, SCAN_LT, A_DK)

    def fwd_spec(col0):
        return pl.BlockSpec(blk, lambda bi, h, j: (bi, j, col0 + h))

    def bwd_spec(col0):
        return pl.BlockSpec(blk, lambda bi, h, j: (bi, nt - 1 - j, col0 + h))

    lb_spec = pl.BlockSpec((lb_fwd.shape[0], A_DK), lambda bi, h, j: (0, h))
    out_shape = jax.ShapeDtypeStruct((b, l, A_W), F32)
    return pl.pallas_call(
        _scan_body,
        grid=(b, A_HEADS, nt),
        in_specs=[lb_spec, lb_spec,
                  fwd_spec(COL_Q), fwd_spec(COL_FF), fwd_spec(COL_I),
                  bwd_spec(COL_Q), bwd_spec(COL_FB), bwd_spec(COL_I)],
        out_specs=[pl.BlockSpec(blk, lambda bi, h, j: (bi, j, h)),
                   pl.BlockSpec(blk, lambda bi, h, j: (bi, nt - 1 - j, h))],
        out_shape=[out_shape, out_shape],
        scratch_shapes=[pltpu.VMEM((A_DV, A_DK), F32), pltpu.VMEM((A_DV, A_DK), F32)],
        compiler_params=pltpu.CompilerParams(
            dimension_semantics=("arbitrary", "arbitrary", "arbitrary"),
            vmem_limit_bytes=VMEM_LIMIT),
        name="gla_scan",
    )(lb_fwd, lb_bwd, proj3, proj3, proj3, proj3, proj3, proj3)


MIX_TM = 256


def _mix_body(x_ref, of_ref, ob_ref, og_ref, u_ref, v_ref, ga0_ref, ga1_ref, gb0_ref, gb1_ref,
              gn_ref, vg_ref, wsp_ref, bsp_ref, woa_ref, wob_ref, wout_ref, g2_ref, wpq_ref,
              keys_ref, x2_ref, xnt_ref, sct_ref):
    o = of_ref[...] + ob_ref[...]
    og = og_ref[...]
    gate_a = og * jax.nn.sigmoid(og)
    heads = []
    for h in range(A_HEADS):
        oh = o[:, h * A_DV:(h + 1) * A_DV]
        heads.append(_rms(oh, gn_ref[...]))
    ya_in = jnp.concatenate(heads, axis=-1) * gate_a
    ya = jnp.dot(ya_in.astype(BF16), woa_ref[...], preferred_element_type=F32)

    u = _gelu_tanh(u_ref[...])
    vv = _rms(_gelu_tanh(v_ref[...]), vg_ref[...]).astype(BF16)
    rows = []
    for c in range(MIX_TM // B_CHUNK):
        groups = []
        for g in range(B_GROUPS):
            vg_blk = vv[c * B_CHUNK:(c + 1) * B_CHUNK, g * B_GC:(g + 1) * B_GC]
            m = jnp.dot(wsp_ref[g], vg_blk, preferred_element_type=F32)
            groups.append(m + bsp_ref[:, g:g + 1])
        rows.append(jnp.concatenate(groups, axis=-1))
    mixed = jnp.concatenate(rows, axis=0)
    yb = jnp.dot((u * mixed).astype(BF16), wob_ref[...], preferred_element_type=F32)

    ga = jnp.concatenate([ga0_ref[...], ga1_ref[...]], axis=-1)
    gb = jnp.concatenate([gb0_ref[...], gb1_ref[...]], axis=-1)
    merged = jax.nn.sigmoid(ga) * ya + jax.nn.sigmoid(gb) * yb
    x2 = x_ref[...] + jnp.dot(merged.astype(BF16), wout_ref[...], preferred_element_type=F32)
    x2_ref[...] = x2

    xn = _rms(x2, g2_ref[...])
    xn_bf = xn.astype(BF16)
    xnt_ref[...] = xn.T.astype(BF16)
    qh = jnp.dot(xn_bf, wpq_ref[...], preferred_element_type=F32).astype(BF16)
    for hp in range(2 * P_HEADS):
        q_hp = qh[:, hp * P_DKEY:(hp + 1) * P_DKEY]
        sct_ref[hp] = lax.dot_general(keys_ref[hp], q_hp, NT_DIMS, preferred_element_type=F32)


def _mix(x2d, proj, o_f, o_b, gn, vg, w_sp, b_sp_t, w_oa, w_ob, w_out, g2, w_pq, keys):
    n = x2d.shape[0]
    tm = MIX_TM

    def col_spec(col0):
        return pl.BlockSpec((tm, 512), lambda i: (i, col0 // 4))

    def full(a):
        return pl.BlockSpec(a.shape, lambda i: (0,) * a.ndim)

    return pl.pallas_call(
        _mix_body,
        grid=(n // tm,),
        in_specs=[
            pl.BlockSpec((tm, D_MODEL), lambda i: (i, 0)),
            pl.BlockSpec((tm, A_W), lambda i: (i, 0)),
            pl.BlockSpec((tm, A_W), lambda i: (i, 0)),
            col_spec(COL_OG), col_spec(COL_U), col_spec(COL_V),
            col_spec(COL_GA), col_spec(COL_GA + 4), col_spec(COL_GB), col_spec(COL_GB + 4),
            full(gn), full(vg), full(w_sp), full(b_sp_t), full(w_oa), full(w_ob), full(w_out),
            full(g2), full(w_pq), full(keys),
        ],
        out_specs=[
            pl.BlockSpec((tm, D_MODEL), lambda i: (i, 0)),
            pl.BlockSpec((D_MODEL, tm), lambda i: (0, i)),
            pl.BlockSpec((2 * P_HEADS, P_NKEYS, tm), lambda i: (0, 0, i)),
        ],
        out_shape=[
            jax.ShapeDtypeStruct((n, D_MODEL), F32),
            jax.ShapeDtypeStruct((D_MODEL, n), BF16),
            jax.ShapeDtypeStruct((2 * P_HEADS, P_NKEYS, n), F32),
        ],
        compiler_params=pltpu.CompilerParams(
            dimension_semantics=("arbitrary",), vmem_limit_bytes=VMEM_LIMIT),
        name="mix",
    )(x2d, o_f, o_b, proj, proj, proj, proj, proj, proj, proj,
      gn, vg, w_sp, b_sp_t, w_oa, w_ob, w_out, g2, w_pq, keys)


SEL_TL = 128
NEG_INF = float("-inf")
NOT_RANKED = 99.0


def _top_values(s, vals_ref):
    work = s
    rank = jnp.full(s.shape, NOT_RANKED, F32)
    for i in range(P_TOPK):
        m = jnp.max(work, axis=0, keepdims=True)
        vals_ref[i:i + 1, :] = m
        hit = work == m
        rank = jnp.where(hit, float(i + 1), rank)
        work = jnp.where(hit, NEG_INF, work)
    return rank


def _select_body(sct_ref, rank_ref, v_ref, n_ref, u_ref, v0_ref, v1_ref):
    for h in range(P_HEADS):
        s0 = sct_ref[2 * h]
        s1 = sct_ref[2 * h + 1]
        rank0 = _top_values(s0, v0_ref)
        rank1 = _top_values(s1, v1_ref)
        top0 = v0_ref[...]
        top1 = v1_ref[...]
        m0 = top0[0:1, :]
        m1 = top1[0:1, :]
        cands = [m0 + top1]
        for i in range(1, P_TOPK):
            cands.append(top0[i:i + 1, :] + top1[0:8, :])
        best = m0 + m1
        z = jnp.zeros_like(best)
        tau = best
        for _ in range(P_TOPK):
            m = functools.reduce(jnp.maximum, [jnp.max(cd, axis=0, keepdims=True) for cd in cands])
            z = z + jnp.exp(m - best)
            tau = m
            cands = [jnp.where(cd == m, NEG_INF, cd) for cd in cands]
        n = jnp.zeros_like(s0)
        for i in range(P_TOPK):
            cnt = jnp.sum(jnp.where(top0[i:i + 1, :] + top1 >= tau, 1.0, 0.0), axis=0, keepdims=True)
            n = jnp.where(rank0 == float(i + 1), cnt, n)
        rank_ref[h] = rank1.astype(BF16)
        v_ref[h] = jnp.exp(s1 - m1).astype(BF16)
        n_ref[h] = n
        u_ref[h] = jnp.exp(s0 - m0) / z


def _select(sct):
    n = sct.shape[-1]
    tl = SEL_TL
    spec = pl.BlockSpec((P_HEADS, P_NKEYS, tl), lambda i: (0, 0, i))
    shape16 = jax.ShapeDtypeStruct((P_HEADS, P_NKEYS, n), BF16)
    shape32 = jax.ShapeDtypeStruct((P_HEADS, P_NKEYS, n), F32)
    return pl.pallas_call(
        _select_body,
        grid=(n // tl,),
        in_specs=[pl.BlockSpec((2 * P_HEADS, P_NKEYS, tl), lambda i: (0, 0, i))],
        out_specs=[spec, spec, spec, spec],
        out_shape=[shape16, shape16, shape32, shape32],
        scratch_shapes=[pltpu.VMEM((P_TOPK, tl), F32), pltpu.VMEM((P_TOPK, tl), F32)],
        compiler_params=pltpu.CompilerParams(
            dimension_semantics=("arbitrary",), vmem_limit_bytes=VMEM_LIMIT),
        name="select",
    )(sct)


PEER_T = 512
PEER_EB = 512
PEER_NI = P_N // PEER_EB
BF16_ROWS = 16


def _peer_body(xnt_ref, rank_ref, v_ref, n_ref, u_ref, pu_ref, pvt_ref, x2_ref, gf_ref,
               y_ref, acc_ref, w_ref, *, n_blocks):
    s = pl.program_id(0)
    i1 = jnp.minimum(s, n_blocks - 1) % PEER_NI
    i3 = jnp.clip(s - 1, 0, n_blocks - 1) % PEER_NI
    cur = s % 2
    prev = 1 - cur

    @pl.when(s == 0)
    def _():
        w_ref[...] = jnp.zeros_like(w_ref)

    @pl.when(i3 == 0)
    def _():
        acc_ref[...] = jnp.zeros_like(acc_ref)

    act = jnp.dot(pu_ref[...], xnt_ref[...], preferred_element_type=F32)
    acc_ref[...] += jnp.dot(pvt_ref[0], w_ref[prev], preferred_element_type=F32)

    groups = P_NKEYS // BF16_ROWS
    for al in range(PEER_EB // P_NKEYS):
        a = i1 * (PEER_EB // P_NKEYS) + al
        g = jnp.zeros((groups, BF16_ROWS, PEER_T), BF16)
        for h in range(P_HEADS):
            nb = jnp.broadcast_to(n_ref[h, pl.ds(a, 1), :], (BF16_ROWS, PEER_T)).astype(BF16)
            ub = jnp.broadcast_to(u_ref[h, pl.ds(a, 1), :], (BF16_ROWS, PEER_T)).astype(BF16)
            r = rank_ref[h].reshape(groups, BF16_ROWS, PEER_T)
            vv = v_ref[h].reshape(groups, BF16_ROWS, PEER_T)
            g = g + jnp.where(r <= nb[None], vv, jnp.zeros_like(vv)) * ub[None]
        rows = slice(al * P_NKEYS, (al + 1) * P_NKEYS)
        w_ref[cur, rows, :] = g.reshape(P_NKEYS, PEER_T) * _gelu_tanh(act[rows, :].astype(BF16))

    @pl.when(jnp.logical_and(s >= 1, i3 == PEER_NI - 1))
    def _():
        y_ref[...] = _rms(x2_ref[...] + acc_ref[...].T, gf_ref[...])


def _peer(xnt, rank1, v, nn, u, pu, pvt3, x2, gf):
    n = x2.shape[0]
    t, eb = PEER_T, PEER_EB
    n_blocks = (n // t) * PEER_NI
    last = n_blocks - 1

    def pair(s, lag):
        return jnp.clip(s - lag, 0, last)

    def sel_spec():
        return pl.BlockSpec((P_HEADS, P_NKEYS, t), lambda s: (0, 0, pair(s, 0) // PEER_NI))

    return pl.pallas_call(
        functools.partial(_peer_body, n_blocks=n_blocks),
        grid=(n_blocks + 1,),
        in_specs=[
            pl.BlockSpec((D_MODEL, t), lambda s: (0, pair(s, 0) // PEER_NI)),
            sel_spec(), sel_spec(), sel_spec(), sel_spec(),
            pl.BlockSpec((eb, D_MODEL), lambda s: (pair(s, 0) % PEER_NI, 0)),
            pl.BlockSpec((1, D_MODEL, eb), lambda s: (pair(s, 1) % PEER_NI, 0, 0)),
            pl.BlockSpec((t, D_MODEL), lambda s: (pair(s, 1) // PEER_NI, 0)),
            pl.BlockSpec((1, D_MODEL), lambda s: (0, 0)),
        ],
        out_specs=pl.BlockSpec((t, D_MODEL), lambda s: (pair(s, 1) // PEER_NI, 0)),
        out_shape=jax.ShapeDtypeStruct((n, D_MODEL), F32),
        scratch_shapes=[pltpu.VMEM((D_MODEL, t), F32),
                        pltpu.VMEM((2, eb, t), BF16)],
        compiler_params=pltpu.CompilerParams(
            dimension_semantics=("arbitrary",), vmem_limit_bytes=VMEM_LIMIT),
        name="peer",
    )(xnt, rank1, v, nn, u, pu, pvt3, x2, gf)


def _trunk(x, p):
    b, l, d = x.shape
    n = b * l
    x2d = x.reshape(n, d)
    proj = _inproj(x2d, p["g1"], p["w_in"])
    o_f, o_b = _scan(proj.reshape(b, l, IN_COLS), p["lb_fwd"], p["lb_bwd"])
    x2, xnt, sct = _mix(x2d, proj, o_f.reshape(n, A_W), o_b.reshape(n, A_W), p["gn"], p["vg"],
                        p["w_sp"], p["b_sp_t"], p["w_oa"], p["w_ob"], p["w_out"], p["g2"],
                        p["w_pq"], p["keys"])
    rank1, v, nn, u = _select(sct)
    y = _peer(xnt, rank1, v, nn, u, p["pu"], p["pvt"], x2, p["gf"])
    return y.reshape(b, l, d)


def kernel(x_prompt, x_sample, norm1_g, w_in, lb_fwd, lb_bwd, gn_a, vnorm_g, w_sp, b_sp, w_oa, w_ob,
           w_out, norm2_g, w_pq, peer_keys, peer_u, peer_v, norm_f):
    layer = 0
    p = {
        "g1": norm1_g[layer].reshape(1, D_MODEL),
        "w_in": w_in[layer].astype(BF16),
        "lb_fwd": lb_fwd,
        "lb_bwd": lb_bwd,
        "gn": gn_a[layer].reshape(1, A_DV),
        "vg": vnorm_g[layer].reshape(1, B_W),
        "w_sp": w_sp[layer].astype(BF16),
        "b_sp_t": b_sp[layer].T,
        "w_oa": w_oa[layer].astype(BF16),
        "w_ob": w_ob[layer].astype(BF16),
        "w_out": w_out[layer].astype(BF16),
        "g2": norm2_g[layer].reshape(1, D_MODEL),
        "w_pq": w_pq[layer].astype(BF16),
        "keys": peer_keys[layer].reshape(2 * P_HEADS, P_NKEYS, P_DKEY).astype(BF16),
        "pu": peer_u[layer].astype(BF16),
        "pvt": peer_v[layer].astype(BF16).reshape(PEER_NI, PEER_EB, D_MODEL).transpose(0, 2, 1),
        "gf": norm_f.reshape(1, D_MODEL),
    }
    return (_trunk(x_prompt, p), _trunk(x_sample, p))
```

```python
import functools

import jax
import jax.numpy as jnp
from jax import lax
from jax.experimental import pallas as pl
from jax.experimental.pallas import tpu as pltpu

F32 = jnp.float32
BF16 = jnp.bfloat16

D_MODEL = 1024
A_HEADS = 4
A_DK = 128
A_DV = 128
A_W = A_HEADS * A_DK
A_CHUNK = 64
B_GROUPS = 4
B_GC = 128
B_W = B_GROUPS * B_GC
B_CHUNK = 128
P_HEADS = 8
P_NKEYS = 128
P_DKEY = 128
P_TOPK = 16
P_N = P_NKEYS * P_NKEYS
EPS = 1e-6
IN_COLS = 3 * A_W + 2 * A_W + 2 * B_W + 2 * D_MODEL
COL_Q, COL_FF, COL_FB, COL_I, COL_OG, COL_U, COL_V, COL_GA, COL_GB = 0, 4, 8, 12, 16, 20, 24, 28, 36

VMEM_LIMIT = 56 * 1024 * 1024

NT_DIMS = (((1,), (1,)), ((), ()))
TN_DIMS = (((0,), (0,)), ((), ()))


def _gelu_tanh(x):
    c = 0.7978845608028654
    return x * (0.5 * (1.0 + jnp.tanh(c * (x + 0.044715 * (x * x * x)))))


def _rms(x, g):
    return x * lax.rsqrt(jnp.mean(x * x, axis=-1, keepdims=True) + EPS) * g


IN_TM = 512
IN_CB = 1408


def _inproj_body(x_ref, g_ref, w_ref, o_ref):
    h = _rms(x_ref[...], g_ref[...])
    o_ref[...] = jnp.dot(h.astype(BF16), w_ref[...], preferred_element_type=F32)


def _inproj(x2d, g1, w_in):
    n = x2d.shape[0]
    return pl.pallas_call(
        _inproj_body,
        grid=(IN_COLS // IN_CB, n // IN_TM),
        in_specs=[
            pl.BlockSpec((IN_TM, D_MODEL), lambda c, i: (i, 0)),
            pl.BlockSpec((1, D_MODEL), lambda c, i: (0, 0)),
            pl.BlockSpec((D_MODEL, IN_CB), lambda c, i: (0, c)),
        ],
        out_specs=pl.BlockSpec((IN_TM, IN_CB), lambda c, i: (i, c)),
        out_shape=jax.ShapeDtypeStruct((n, IN_COLS), F32),
        compiler_params=pltpu.CompilerParams(
            dimension_semantics=("arbitrary", "arbitrary"), vmem_limit_bytes=VMEM_LIMIT),
        name="inproj",
    )(x2d, g1, w_in)


SCAN_LT = 512


def _first_softmax_row(lb_ref):
    lb = lb_ref[...]
    m = jnp.max(lb, axis=0, keepdims=True)
    e = jnp.exp(lb - m)
    return e[0:1, :] / jnp.sum(e, axis=0, keepdims=True)


def _scan_body(lbf_ref, lbb_ref, tri_ref, qf_ref, ff_ref, vf_ref, qb_ref, fb_ref, vb_ref,
               of_ref, ob_ref, sf_ref, sb_ref):
    @pl.when(pl.program_id(2) == 0)
    def _():
        sf_ref[...] = jnp.zeros_like(sf_ref)
        sb_ref[...] = jnp.zeros_like(sb_ref)

    n_chunks = SCAN_LT // A_CHUNK
    row = lax.broadcasted_iota(jnp.int32, (A_CHUNK, A_CHUNK), 0)
    col = lax.broadcasted_iota(jnp.int32, (A_CHUNK, A_CHUNK), 1)

    def rows(x, c):
        return x[c * A_CHUNK:(c + 1) * A_CHUNK, :]

    def prepare(q_ref, f_ref, v_ref, lb_ref, tri, last_row):
        lb = _first_softmax_row(lb_ref)
        qr = q_ref[...]
        q = qr * jax.nn.sigmoid(qr)
        f = lb + (1.0 - lb) * jax.nn.sigmoid(f_ref[...])
        logf = jnp.log(f)
        k = 1.0 - f
        hi = logf.astype(BF16)
        rem = logf - hi.astype(F32)
        mid = rem.astype(BF16)
        lo = (rem - mid.astype(F32)).astype(BF16)
        g3 = jnp.dot(tri, jnp.concatenate([hi, mid, lo], axis=-1), preferred_element_type=F32)
        g = (g3[:, :A_DK] + g3[:, A_DK:2 * A_DK]) + g3[:, 2 * A_DK:]
        q_dec = (q * jnp.exp(g)).astype(BF16)
        k_inv = (k * jnp.exp(-g)).astype(BF16)
        g_last = [rows(g, c)[last_row:last_row + 1, :] for c in range(n_chunks)]
        k_end = [(rows(k, c) * jnp.exp(g_last[c] - rows(g, c))).astype(BF16) for c in range(n_chunks)]
        return dict(q_dec=q_dec, k_inv=k_inv, k_end=k_end, v=v_ref[...].astype(BF16),
                    decay=[jnp.exp(gl) for gl in g_last])

    dirs = [
        dict(p=prepare(qf_ref, ff_ref, vf_ref, lbf_ref, tri_ref[0], A_CHUNK - 1),
             mask=col <= row, order=list(range(n_chunks)), o_ref=of_ref, s_ref=sf_ref),
        dict(p=prepare(qb_ref, fb_ref, vb_ref, lbb_ref, tri_ref[1], 0),
             mask=col >= row, order=list(reversed(range(n_chunks))), o_ref=ob_ref, s_ref=sb_ref),
    ]
    for d in dirs:
        p = d["p"]
        d["att"] = [lax.dot_general(rows(p["q_dec"], c), rows(p["k_inv"], c), NT_DIMS,
                                    preferred_element_type=F32) for c in range(n_chunks)]
        d["ds"] = [lax.dot_general(rows(p["v"], c), p["k_end"][c], TN_DIMS,
                                   preferred_element_type=F32) for c in range(n_chunks)]
    for d in dirs:
        p = d["p"]
        att = [jnp.where(d["mask"], a, 0.0).astype(BF16) for a in d["att"]]
        d["o"] = [jnp.dot(att[c], rows(p["v"], c), preferred_element_type=F32)
                  for c in range(n_chunks)]
    for d in dirs:
        p = d["p"]
        s_t = d["s_ref"][...]
        entering = {}
        for c in d["order"]:
            entering[c] = s_t.astype(BF16)
            s_t = s_t * p["decay"][c] + d["ds"][c]
        d["s_ref"][...] = s_t
        for c in range(n_chunks):
            o = d["o"][c] + lax.dot_general(rows(p["q_dec"], c), entering[c], NT_DIMS,
                                            preferred_element_type=F32)
            d["o_ref"][c * A_CHUNK:(c + 1) * A_CHUNK, :] = o


def _block_triangles():
    r = jnp.arange(SCAN_LT)[:, None]
    c = jnp.arange(SCAN_LT)[None, :]
    same = (r // A_CHUNK) == (c // A_CHUNK)
    return jnp.stack([same & (c <= r), same & (c >= r)]).astype(BF16)


def _scan(proj3, lb_fwd, lb_bwd):
    b, l, _ = proj3.shape
    nt = l // SCAN_LT
    blk = (None, SCAN_LT, A_DK)

    def fwd_spec(col0):
        return pl.BlockSpec(blk, lambda bi, h, j: (bi, j, col0 + h))

    def bwd_spec(col0):
        return pl.BlockSpec(blk, lambda bi, h, j: (bi, nt - 1 - j, col0 + h))

    lb_spec = pl.BlockSpec((lb_fwd.shape[0], A_DK), lambda bi, h, j: (0, h))
    tri_spec = pl.BlockSpec((2, SCAN_LT, SCAN_LT), lambda bi, h, j: (0, 0, 0))
    out_shape = jax.ShapeDtypeStruct((b, l, A_W), F32)
    return pl.pallas_call(
        _scan_body,
        grid=(b, A_HEADS, nt),
        in_specs=[lb_spec, lb_spec, tri_spec,
                  fwd_spec(COL_Q), fwd_spec(COL_FF), fwd_spec(COL_I),
                  bwd_spec(COL_Q), bwd_spec(COL_FB), bwd_spec(COL_I)],
        out_specs=[pl.BlockSpec(blk, lambda bi, h, j: (bi, j, h)),
                   pl.BlockSpec(blk, lambda bi, h, j: (bi, nt - 1 - j, h))],
        out_shape=[out_shape, out_shape],
        scratch_shapes=[pltpu.VMEM((A_DV, A_DK), F32), pltpu.VMEM((A_DV, A_DK), F32)],
        compiler_params=pltpu.CompilerParams(
            dimension_semantics=("arbitrary", "arbitrary", "arbitrary"),
            vmem_limit_bytes=VMEM_LIMIT),
        name="gla_scan",
    )(lb_fwd, lb_bwd, _block_triangles(), proj3, proj3, proj3, proj3, proj3, proj3)


MIX_TM = 256


def _mix_body(x_ref, of_ref, ob_ref, og_ref, u_ref, v_ref, ga0_ref, ga1_ref, gb0_ref, gb1_ref,
              gn_ref, vg_ref, wsp_ref, bsp_ref, woa_ref, wob_ref, wout_ref, g2_ref, wpq_ref,
              keys_ref, x2_ref, xnt_ref, sct_ref):
    o = of_ref[...] + ob_ref[...]
    og = og_ref[...]
    gate_a = og * jax.nn.sigmoid(og)
    heads = []
    for h in range(A_HEADS):
        oh = o[:, h * A_DV:(h + 1) * A_DV]
        heads.append(_rms(oh, gn_ref[...]))
    ya_in = jnp.concatenate(heads, axis=-1) * gate_a
    ya = jnp.dot(ya_in.astype(BF16), woa_ref[...], preferred_element_type=F32)

    u = _gelu_tanh(u_ref[...])
    vv = _rms(_gelu_tanh(v_ref[...]), vg_ref[...]).astype(BF16)
    rows = []
    for c in range(MIX_TM // B_CHUNK):
        groups = []
        for g in range(B_GROUPS):
            vg_blk = vv[c * B_CHUNK:(c + 1) * B_CHUNK, g * B_GC:(g + 1) * B_GC]
            m = jnp.dot(wsp_ref[g], vg_blk, preferred_element_type=F32)
            groups.append(m + bsp_ref[:, g:g + 1])
        rows.append(jnp.concatenate(groups, axis=-1))
    mixed = jnp.concatenate(rows, axis=0)
    yb = jnp.dot((u * mixed).astype(BF16), wob_ref[...], preferred_element_type=F32)

    ga = jnp.concatenate([ga0_ref[...], ga1_ref[...]], axis=-1)
    gb = jnp.concatenate([gb0_ref[...], gb1_ref[...]], axis=-1)
    merged = jax.nn.sigmoid(ga) * ya + jax.nn.sigmoid(gb) * yb
    x2 = x_ref[...] + jnp.dot(merged.astype(BF16), wout_ref[...], preferred_element_type=F32)
    x2_ref[...] = x2

    xn = _rms(x2, g2_ref[...])
    xn_bf = xn.astype(BF16)
    xnt_ref[...] = xn.T.astype(BF16)
    qh = jnp.dot(xn_bf, wpq_ref[...], preferred_element_type=F32).astype(BF16)
    for hp in range(2 * P_HEADS):
        q_hp = qh[:, hp * P_DKEY:(hp + 1) * P_DKEY]
        sct_ref[hp] = lax.dot_general(keys_ref[hp], q_hp, NT_DIMS, preferred_element_type=F32)


def _mix(x2d, proj, o_f, o_b, gn, vg, w_sp, b_sp_t, w_oa, w_ob, w_out, g2, w_pq, keys):
    n = x2d.shape[0]
    tm = MIX_TM

    def col_spec(col0):
        return pl.BlockSpec((tm, 512), lambda i: (i, col0 // 4))

    def full(a):
        return pl.BlockSpec(a.shape, lambda i: (0,) * a.ndim)

    return pl.pallas_call(
        _mix_body,
        grid=(n // tm,),
        in_specs=[
            pl.BlockSpec((tm, D_MODEL), lambda i: (i, 0)),
            pl.BlockSpec((tm, A_W), lambda i: (i, 0)),
            pl.BlockSpec((tm, A_W), lambda i: (i, 0)),
            col_spec(COL_OG), col_spec(COL_U), col_spec(COL_V),
            col_spec(COL_GA), col_spec(COL_GA + 4), col_spec(COL_GB), col_spec(COL_GB + 4),
            full(gn), full(vg), full(w_sp), full(b_sp_t), full(w_oa), full(w_ob), full(w_out),
            full(g2), full(w_pq), full(keys),
        ],
        out_specs=[
            pl.BlockSpec((tm, D_MODEL), lambda i: (i, 0)),
            pl.BlockSpec((D_MODEL, tm), lambda i: (0, i)),
            pl.BlockSpec((2 * P_HEADS, P_NKEYS, tm), lambda i: (0, 0, i)),
        ],
        out_shape=[
            jax.ShapeDtypeStruct((n, D_MODEL), F32),
            jax.ShapeDtypeStruct((D_MODEL, n), BF16),
            jax.ShapeDtypeStruct((2 * P_HEADS, P_NKEYS, n), F32),
        ],
        compiler_params=pltpu.CompilerParams(
            dimension_semantics=("arbitrary",), vmem_limit_bytes=VMEM_LIMIT),
        name="mix",
    )(x2d, o_f, o_b, proj, proj, proj, proj, proj, proj, proj,
      gn, vg, w_sp, b_sp_t, w_oa, w_ob, w_out, g2, w_pq, keys)


SEL_TL = 128
NEG_INF = float("-inf")
NOT_RANKED = float(P_TOPK + 1)
SUBLANES = 8


def _sorting_network(n):
    pairs = []
    p = 1
    while p < n:
        k = p
        while k >= 1:
            for j in range(k % p, n - k, 2 * k):
                for i in range(min(k, n - j - k)):
                    if (i + j) // (2 * p) == (i + j + k) // (2 * p):
                        pairs.append((i + j, i + j + k))
            k //= 2
        p *= 2
    return pairs


def _pop_top(levels, n_out, on_value):
    levels = list(levels)
    for i in range(n_out):
        m = jnp.max(levels[0], axis=0, keepdims=True)
        on_value(i, m)
        hit = levels[0] == m
        for k in range(min(len(levels), n_out - 1 - i)):
            below = levels[k + 1] if k + 1 < len(levels) else NEG_INF
            levels[k] = jnp.where(hit, below, levels[k])


def _top_values(s, vals_ref):
    cols = [s[k * SUBLANES:(k + 1) * SUBLANES, :] for k in range(P_NKEYS // SUBLANES)]
    for i, j in _sorting_network(len(cols)):
        cols[i], cols[j] = jnp.maximum(cols[i], cols[j]), jnp.minimum(cols[i], cols[j])

    def store(i, m):
        vals_ref[i:i + 1, :] = m

    _pop_top(cols, P_TOPK, store)


def _select_body(sct_ref, rank_ref, v_ref, n_ref, u_ref, v0_ref, v1_ref):
    for h in range(P_HEADS):
        s0 = sct_ref[2 * h]
        s1 = sct_ref[2 * h + 1]
        _top_values(s0, v0_ref)
        _top_values(s1, v1_ref)
        top0 = v0_ref[...]
        top1 = v1_ref[...]
        best = top0[0:1, :] + top1[0:1, :]
        lo_levels = [top0[0:SUBLANES, :] + top1[j:j + 1, :] for j in range(P_TOPK)]
        hi_level0 = top0[SUBLANES:, :] + top1[0:1, :]
        levels = [jnp.concatenate([lo_levels[0], hi_level0], axis=0)] + [
            jnp.concatenate([lv, jnp.full_like(lv, NEG_INF)], axis=0) for lv in lo_levels[1:]]
        stats = {"z": jnp.zeros_like(best), "tau": best}

        def accumulate(i, m, stats=stats, best=best):
            stats["z"] = stats["z"] + jnp.exp(m - best)
            stats["tau"] = m

        _pop_top(levels, P_TOPK, accumulate)
        tau = stats["tau"]
        n = jnp.zeros_like(s0)
        for j in range(P_TOPK):
            n = jnp.where(s0 + top1[j:j + 1, :] >= tau, float(j + 1), n)
        rank1 = jnp.full(s1.shape, NOT_RANKED, F32)
        for j in reversed(range(P_TOPK)):
            rank1 = jnp.where(s1 >= top1[j:j + 1, :], float(j + 1), rank1)
        rank_ref[h] = rank1.astype(BF16)
        v_ref[h] = jnp.exp(s1 - top1[0:1, :]).astype(BF16)
        n_ref[h] = n
        u_ref[h] = jnp.exp(s0 - top0[0:1, :]) / stats["z"]


def _select(sct):
    n = sct.shape[-1]
    tl = SEL_TL
    spec = pl.BlockSpec((P_HEADS, P_NKEYS, tl), lambda i: (0, 0, i))
    shape16 = jax.ShapeDtypeStruct((P_HEADS, P_NKEYS, n), BF16)
    shape32 = jax.ShapeDtypeStruct((P_HEADS, P_NKEYS, n), F32)
    return pl.pallas_call(
        _select_body,
        grid=(n // tl,),
        in_specs=[pl.BlockSpec((2 * P_HEADS, P_NKEYS, tl), lambda i: (0, 0, i))],
        out_specs=[spec, spec, spec, spec],
        out_shape=[shape16, shape16, shape32, shape32],
        scratch_shapes=[pltpu.VMEM((P_TOPK, tl), F32), pltpu.VMEM((P_TOPK, tl), F32)],
        compiler_params=pltpu.CompilerParams(
            dimension_semantics=("arbitrary",), vmem_limit_bytes=VMEM_LIMIT),
        name="select",
    )(sct)


PEER_T = 512
PEER_EB = 512
PEER_NI = P_N // PEER_EB
BF16_ROWS = 16


def _peer_body(xnt_ref, rank_ref, v_ref, n_ref, u_ref, pu_ref, pvt_ref, x2_ref, gf_ref,
               y_ref, acc_ref, w_ref, *, n_blocks):
    s = pl.program_id(0)
    i1 = jnp.minimum(s, n_blocks - 1) % PEER_NI
    i3 = jnp.clip(s - 1, 0, n_blocks - 1) % PEER_NI
    cur = s % 2
    prev = 1 - cur

    @pl.when(s == 0)
    def _():
        w_ref[...] = jnp.zeros_like(w_ref)

    @pl.when(i3 == 0)
    def _():
        acc_ref[...] = jnp.zeros_like(acc_ref)

    act = jnp.dot(pu_ref[...], xnt_ref[...], preferred_element_type=F32)
    acc_ref[...] += jnp.dot(pvt_ref[0], w_ref[prev], preferred_element_type=F32)

    groups = P_NKEYS // BF16_ROWS
    for al in range(PEER_EB // P_NKEYS):
        a = i1 * (PEER_EB // P_NKEYS) + al
        g = jnp.zeros((groups, BF16_ROWS, PEER_T), BF16)
        for h in range(P_HEADS):
            nb = jnp.broadcast_to(n_ref[h, pl.ds(a, 1), :], (BF16_ROWS, PEER_T)).astype(BF16)
            ub = jnp.broadcast_to(u_ref[h, pl.ds(a, 1), :], (BF16_ROWS, PEER_T)).astype(BF16)
            r = rank_ref[h].reshape(groups, BF16_ROWS, PEER_T)
            vv = v_ref[h].reshape(groups, BF16_ROWS, PEER_T)
            g = g + jnp.where(r <= nb[None], vv, jnp.zeros_like(vv)) * ub[None]
        rows = slice(al * P_NKEYS, (al + 1) * P_NKEYS)
        w_ref[cur, rows, :] = g.reshape(P_NKEYS, PEER_T) * _gelu_tanh(act[rows, :].astype(BF16))

    @pl.when(jnp.logical_and(s >= 1, i3 == PEER_NI - 1))
    def _():
        y_ref[...] = _rms(x2_ref[...] + acc_ref[...].T, gf_ref[...])


def _peer(xnt, rank1, v, nn, u, pu, pvt3, x2, gf):
    n = x2.shape[0]
    t, eb = PEER_T, PEER_EB
    n_blocks = (n // t) * PEER_NI
    last = n_blocks - 1

    def pair(s, lag):
        return jnp.clip(s - lag, 0, last)

    def sel_spec():
        return pl.BlockSpec((P_HEADS, P_NKEYS, t), lambda s: (0, 0, pair(s, 0) // PEER_NI))

    return pl.pallas_call(
        functools.partial(_peer_body, n_blocks=n_blocks),
        grid=(n_blocks + 1,),
        in_specs=[
            pl.BlockSpec((D_MODEL, t), lambda s: (0, pair(s, 0) // PEER_NI)),
            sel_spec(), sel_spec(), sel_spec(), sel_spec(),
            pl.BlockSpec((eb, D_MODEL), lambda s: (pair(s, 0) % PEER_NI, 0)),
            pl.BlockSpec((1, D_MODEL, eb), lambda s: (pair(s, 1) % PEER_NI, 0, 0)),
            pl.BlockSpec((t, D_MODEL), lambda s: (pair(s, 1) // PEER_NI, 0)),
            pl.BlockSpec((1, D_MODEL), lambda s: (0, 0)),
        ],
        out_specs=pl.BlockSpec((t, D_MODEL), lambda s: (pair(s, 1) // PEER_NI, 0)),
        out_shape=jax.ShapeDtypeStruct((n, D_MODEL), F32),
        scratch_shapes=[pltpu.VMEM((D_MODEL, t), F32),
                        pltpu.VMEM((2, eb, t), BF16)],
        compiler_params=pltpu.CompilerParams(
            dimension_semantics=("arbitrary",), vmem_limit_bytes=VMEM_LIMIT),
        name="peer",
    )(xnt, rank1, v, nn, u, pu, pvt3, x2, gf)


def _trunk(x, p):
    b, l, d = x.shape
    n = b * l
    x2d = x.reshape(n, d)
    proj = _inproj(x2d, p["g1"], p["w_in"])
    o_f, o_b = _scan(proj.reshape(b, l, IN_COLS), p["lb_fwd"], p["lb_bwd"])
    x2, xnt, sct = _mix(x2d, proj, o_f.reshape(n, A_W), o_b.reshape(n, A_W), p["gn"], p["vg"],
                        p["w_sp"], p["b_sp_t"], p["w_oa"], p["w_ob"], p["w_out"], p["g2"],
                        p["w_pq"], p["keys"])
    rank1, v, nn, u = _select(sct)
    y = _peer(xnt, rank1, v, nn, u, p["pu"], p["pvt"], x2, p["gf"])
    return y.reshape(b, l, d)


def kernel(x_prompt, x_sample, norm1_g, w_in, lb_fwd, lb_bwd, gn_a, vnorm_g, w_sp, b_sp, w_oa, w_ob,
           w_out, norm2_g, w_pq, peer_keys, peer_u, peer_v, norm_f):
    layer = 0
    p = {
        "g1": norm1_g[layer].reshape(1, D_MODEL),
        "w_in": w_in[layer].astype(BF16),
        "lb_fwd": lb_fwd,
        "lb_bwd": lb_bwd,
        "gn": gn_a[layer].reshape(1, A_DV),
        "vg": vnorm_g[layer].reshape(1, B_W),
        "w_sp": w_sp[layer].astype(BF16),
        "b_sp_t": b_sp[layer].T,
        "w_oa": w_oa[layer].astype(BF16),
        "w_ob": w_ob[layer].astype(BF16),
        "w_out": w_out[layer].astype(BF16),
        "g2": norm2_g[layer].reshape(1, D_MODEL),
        "w_pq": w_pq[layer].astype(BF16),
        "keys": peer_keys[layer].reshape(2 * P_HEADS, P_NKEYS, P_DKEY).astype(BF16),
        "pu": peer_u[layer].astype(BF16),
        "pvt": peer_v[layer].astype(BF16).reshape(PEER_NI, PEER_EB, D_MODEL).transpose(0, 2, 1),
        "gf": norm_f.reshape(1, D_MODEL),
    }
    return (_trunk(x_prompt, p), _trunk(x_sample, p))
```

```python
import functools

import jax
import jax.numpy as jnp
from jax import lax
from jax.experimental import pallas as pl
from jax.experimental.pallas import tpu as pltpu

F32 = jnp.float32
BF16 = jnp.bfloat16

D_MODEL = 1024
A_HEADS = 4
A_DK = 128
A_DV = 128
A_W = A_HEADS * A_DK
A_CHUNK = 64
B_GROUPS = 4
B_GC = 128
B_W = B_GROUPS * B_GC
B_CHUNK = 128
P_HEADS = 8
P_NKEYS = 128
P_DKEY = 128
P_TOPK = 16
P_N = P_NKEYS * P_NKEYS
EPS = 1e-6
IN_COLS = 3 * A_W + 2 * A_W + 2 * B_W + 2 * D_MODEL
COL_Q, COL_FF, COL_FB, COL_I, COL_OG, COL_U, COL_V, COL_GA, COL_GB = 0, 4, 8, 12, 16, 20, 24, 28, 36

VMEM_LIMIT = 56 * 1024 * 1024

NT_DIMS = (((1,), (1,)), ((), ()))
TN_DIMS = (((0,), (0,)), ((), ()))


def _gelu_tanh(x):
    c = 0.7978845608028654
    return x * (0.5 * (1.0 + jnp.tanh(c * (x + 0.044715 * (x * x * x)))))


def _rms(x, g):
    return x * lax.rsqrt(jnp.mean(x * x, axis=-1, keepdims=True) + EPS) * g


IN_TM = 512
IN_CB = 1408


def _inproj_body(x_ref, g_ref, w_ref, o_ref):
    h = _rms(x_ref[...], g_ref[...])
    o_ref[...] = jnp.dot(h.astype(BF16), w_ref[...], preferred_element_type=F32)


def _inproj(x2d, g1, w_in):
    n = x2d.shape[0]
    return pl.pallas_call(
        _inproj_body,
        grid=(IN_COLS // IN_CB, n // IN_TM),
        in_specs=[
            pl.BlockSpec((IN_TM, D_MODEL), lambda c, i: (i, 0)),
            pl.BlockSpec((1, D_MODEL), lambda c, i: (0, 0)),
            pl.BlockSpec((D_MODEL, IN_CB), lambda c, i: (0, c)),
        ],
        out_specs=pl.BlockSpec((IN_TM, IN_CB), lambda c, i: (i, c)),
        out_shape=jax.ShapeDtypeStruct((n, IN_COLS), F32),
        compiler_params=pltpu.CompilerParams(
            dimension_semantics=("arbitrary", "arbitrary"), vmem_limit_bytes=VMEM_LIMIT),
        name="inproj",
    )(x2d, g1, w_in)


SCAN_LT = 512


def _first_softmax_row(lb_ref):
    lb = lb_ref[...]
    m = jnp.max(lb, axis=0, keepdims=True)
    e = jnp.exp(lb - m)
    return e[0:1, :] / jnp.sum(e, axis=0, keepdims=True)


def _scan_body(lbf_ref, lbb_ref, tri_ref, qf_ref, ff_ref, vf_ref, qb_ref, fb_ref, vb_ref,
               of_ref, ob_ref, sf_ref, sb_ref):
    @pl.when(pl.program_id(2) == 0)
    def _():
        sf_ref[...] = jnp.zeros_like(sf_ref)
        sb_ref[...] = jnp.zeros_like(sb_ref)

    n_chunks = SCAN_LT // A_CHUNK
    row = lax.broadcasted_iota(jnp.int32, (A_CHUNK, A_CHUNK), 0)
    col = lax.broadcasted_iota(jnp.int32, (A_CHUNK, A_CHUNK), 1)

    def rows(x, c):
        return x[c * A_CHUNK:(c + 1) * A_CHUNK, :]

    def prepare(q_ref, f_ref, v_ref, lb_ref, tri, last_row):
        lb = _first_softmax_row(lb_ref)
        qr = q_ref[...]
        q = qr * jax.nn.sigmoid(qr)
        f = lb + (1.0 - lb) * jax.nn.sigmoid(f_ref[...])
        logf = jnp.log(f)
        k = 1.0 - f
        hi = logf.astype(BF16)
        rem = logf - hi.astype(F32)
        mid = rem.astype(BF16)
        lo = (rem - mid.astype(F32)).astype(BF16)
        g3 = jnp.dot(tri, jnp.concatenate([hi, mid, lo], axis=-1), preferred_element_type=F32)
        g = (g3[:, :A_DK] + g3[:, A_DK:2 * A_DK]) + g3[:, 2 * A_DK:]
        q_dec = (q * jnp.exp(g)).astype(BF16)
        k_inv = (k * jnp.exp(-g)).astype(BF16)
        g_last = [rows(g, c)[last_row:last_row + 1, :] for c in range(n_chunks)]
        k_end = [(rows(k, c) * jnp.exp(g_last[c] - rows(g, c))).astype(BF16) for c in range(n_chunks)]
        return dict(q_dec=q_dec, k_inv=k_inv, k_end=k_end, v=v_ref[...].astype(BF16),
                    decay=[jnp.exp(gl) for gl in g_last])

    dirs = [
        dict(p=prepare(qf_ref, ff_ref, vf_ref, lbf_ref, tri_ref[0], A_CHUNK - 1),
             mask=col <= row, order=list(range(n_chunks)), o_ref=of_ref, s_ref=sf_ref),
        dict(p=prepare(qb_ref, fb_ref, vb_ref, lbb_ref, tri_ref[1], 0),
             mask=col >= row, order=list(reversed(range(n_chunks))), o_ref=ob_ref, s_ref=sb_ref),
    ]
    for d in dirs:
        p = d["p"]
        d["att"] = [lax.dot_general(rows(p["q_dec"], c), rows(p["k_inv"], c), NT_DIMS,
                                    preferred_element_type=F32) for c in range(n_chunks)]
        d["ds"] = [lax.dot_general(rows(p["v"], c), p["k_end"][c], TN_DIMS,
                                   preferred_element_type=F32) for c in range(n_chunks)]
    for d in dirs:
        p = d["p"]
        att = [jnp.where(d["mask"], a, 0.0).astype(BF16) for a in d["att"]]
        d["o"] = [jnp.dot(att[c], rows(p["v"], c), preferred_element_type=F32)
                  for c in range(n_chunks)]
    for d in dirs:
        p = d["p"]
        s_t = d["s_ref"][...]
        entering = {}
        for c in d["order"]:
            entering[c] = s_t.astype(BF16)
            s_t = s_t * p["decay"][c] + d["ds"][c]
        d["s_ref"][...] = s_t
        for c in range(n_chunks):
            o = d["o"][c] + lax.dot_general(rows(p["q_dec"], c), entering[c], NT_DIMS,
                                            preferred_element_type=F32)
            d["o_ref"][c * A_CHUNK:(c + 1) * A_CHUNK, :] = o


def _block_triangles():
    r = jnp.arange(SCAN_LT)[:, None]
    c = jnp.arange(SCAN_LT)[None, :]
    same = (r // A_CHUNK) == (c // A_CHUNK)
    return jnp.stack([same & (c <= r), same & (c >= r)]).astype(BF16)


def _scan(proj3, lb_fwd, lb_bwd):
    b, l, _ = proj3.shape
    nt = l // SCAN_LT
    blk = (None, SCAN_LT, A_DK)

    def fwd_spec(col0):
        return pl.BlockSpec(blk, lambda bi, h, j: (bi, j, col0 + h))

    def bwd_spec(col0):
        return pl.BlockSpec(blk, lambda bi, h, j: (bi, nt - 1 - j, col0 + h))

    lb_spec = pl.BlockSpec((lb_fwd.shape[0], A_DK), lambda bi, h, j: (0, h))
    tri_spec = pl.BlockSpec((2, SCAN_LT, SCAN_LT), lambda bi, h, j: (0, 0, 0))
    out_shape = jax.ShapeDtypeStruct((b, l, A_W), F32)
    return pl.pallas_call(
        _scan_body,
        grid=(b, A_HEADS, nt),
        in_specs=[lb_spec, lb_spec, tri_spec,
                  fwd_spec(COL_Q), fwd_spec(COL_FF), fwd_spec(COL_I),
                  bwd_spec(COL_Q), bwd_spec(COL_FB), bwd_spec(COL_I)],
        out_specs=[pl.BlockSpec(blk, lambda bi, h, j: (bi, j, h)),
                   pl.BlockSpec(blk, lambda bi, h, j: (bi, nt - 1 - j, h))],
        out_shape=[out_shape, out_shape],
        scratch_shapes=[pltpu.VMEM((A_DV, A_DK), F32), pltpu.VMEM((A_DV, A_DK), F32)],
        compiler_params=pltpu.CompilerParams(
            dimension_semantics=("arbitrary", "arbitrary", "arbitrary"),
            vmem_limit_bytes=VMEM_LIMIT),
        name="gla_scan",
    )(lb_fwd, lb_bwd, _block_triangles(), proj3, proj3, proj3, proj3, proj3, proj3)


MIX_TM = 256


def _mix_body(x_ref, of_ref, ob_ref, og_ref, u_ref, v_ref, ga0_ref, ga1_ref, gb0_ref, gb1_ref,
              gn_ref, vg_ref, wsp_ref, bsp_ref, woa_ref, wob_ref, wout_ref, g2_ref, wpq_ref,
              keys_ref, x2_ref, xnt_ref, sct_ref):
    o = of_ref[...] + ob_ref[...]
    og = og_ref[...]
    gate_a = og * jax.nn.sigmoid(og)
    heads = []
    for h in range(A_HEADS):
        oh = o[:, h * A_DV:(h + 1) * A_DV]
        heads.append(_rms(oh, gn_ref[...]))
    ya_in = jnp.concatenate(heads, axis=-1) * gate_a
    ya = jnp.dot(ya_in.astype(BF16), woa_ref[...], preferred_element_type=F32)

    u = _gelu_tanh(u_ref[...])
    vv = _rms(_gelu_tanh(v_ref[...]), vg_ref[...]).astype(BF16)
    rows = []
    for c in range(MIX_TM // B_CHUNK):
        groups = []
        for g in range(B_GROUPS):
            vg_blk = vv[c * B_CHUNK:(c + 1) * B_CHUNK, g * B_GC:(g + 1) * B_GC]
            m = jnp.dot(wsp_ref[g], vg_blk, preferred_element_type=F32)
            groups.append(m + bsp_ref[:, g:g + 1])
        rows.append(jnp.concatenate(groups, axis=-1))
    mixed = jnp.concatenate(rows, axis=0)
    yb = jnp.dot((u * mixed).astype(BF16), wob_ref[...], preferred_element_type=F32)

    ga = jnp.concatenate([ga0_ref[...], ga1_ref[...]], axis=-1)
    gb = jnp.concatenate([gb0_ref[...], gb1_ref[...]], axis=-1)
    merged = jax.nn.sigmoid(ga) * ya + jax.nn.sigmoid(gb) * yb
    x2 = x_ref[...] + jnp.dot(merged.astype(BF16), wout_ref[...], preferred_element_type=F32)
    x2_ref[...] = x2

    xn = _rms(x2, g2_ref[...])
    xn_bf = xn.astype(BF16)
    xnt_ref[...] = xn.T.astype(BF16)
    qh = jnp.dot(xn_bf, wpq_ref[...], preferred_element_type=F32).astype(BF16)
    for hp in range(2 * P_HEADS):
        q_hp = qh[:, hp * P_DKEY:(hp + 1) * P_DKEY]
        sct_ref[hp] = lax.dot_general(keys_ref[hp], q_hp, NT_DIMS, preferred_element_type=F32)


def _mix(x2d, proj, o_f, o_b, gn, vg, w_sp, b_sp_t, w_oa, w_ob, w_out, g2, w_pq, keys):
    n = x2d.shape[0]
    tm = MIX_TM

    def col_spec(col0):
        return pl.BlockSpec((tm, 512), lambda i: (i, col0 // 4))

    def full(a):
        return pl.BlockSpec(a.shape, lambda i: (0,) * a.ndim)

    return pl.pallas_call(
        _mix_body,
        grid=(n // tm,),
        in_specs=[
            pl.BlockSpec((tm, D_MODEL), lambda i: (i, 0)),
            pl.BlockSpec((tm, A_W), lambda i: (i, 0)),
            pl.BlockSpec((tm, A_W), lambda i: (i, 0)),
            col_spec(COL_OG), col_spec(COL_U), col_spec(COL_V),
            col_spec(COL_GA), col_spec(COL_GA + 4), col_spec(COL_GB), col_spec(COL_GB + 4),
            full(gn), full(vg), full(w_sp), full(b_sp_t), full(w_oa), full(w_ob), full(w_out),
            full(g2), full(w_pq), full(keys),
        ],
        out_specs=[
            pl.BlockSpec((tm, D_MODEL), lambda i: (i, 0)),
            pl.BlockSpec((D_MODEL, tm), lambda i: (0, i)),
            pl.BlockSpec((2 * P_HEADS, P_NKEYS, tm), lambda i: (0, 0, i)),
        ],
        out_shape=[
            jax.ShapeDtypeStruct((n, D_MODEL), F32),
            jax.ShapeDtypeStruct((D_MODEL, n), BF16),
            jax.ShapeDtypeStruct((2 * P_HEADS, P_NKEYS, n), F32),
        ],
        compiler_params=pltpu.CompilerParams(
            dimension_semantics=("arbitrary",), vmem_limit_bytes=VMEM_LIMIT),
        name="mix",
    )(x2d, o_f, o_b, proj, proj, proj, proj, proj, proj, proj,
      gn, vg, w_sp, b_sp_t, w_oa, w_ob, w_out, g2, w_pq, keys)


SEL_TL = 128
NEG_INF = float("-inf")
NOT_RANKED = float(P_TOPK + 1)
SUBLANES = 8


def _sorting_network(n):
    pairs = []
    p = 1
    while p < n:
        k = p
        while k >= 1:
            for j in range(k % p, n - k, 2 * k):
                for i in range(min(k, n - j - k)):
                    if (i + j) // (2 * p) == (i + j + k) // (2 * p):
                        pairs.append((i + j, i + j + k))
            k //= 2
        p *= 2
    return pairs


def _pop_top(levels, n_out, on_value):
    levels = list(levels)
    for i in range(n_out):
        m = jnp.max(levels[0], axis=0, keepdims=True)
        on_value(i, m)
        hit = levels[0] == m
        for k in range(min(len(levels), n_out - 1 - i)):
            below = levels[k + 1] if k + 1 < len(levels) else NEG_INF
            levels[k] = jnp.where(hit, below, levels[k])


def _top_values(s, vals_ref):
    cols = [s[k * SUBLANES:(k + 1) * SUBLANES, :] for k in range(P_NKEYS // SUBLANES)]
    for i, j in _sorting_network(len(cols)):
        cols[i], cols[j] = jnp.maximum(cols[i], cols[j]), jnp.minimum(cols[i], cols[j])

    def store(i, m):
        vals_ref[i:i + 1, :] = m

    _pop_top(cols, P_TOPK, store)


def _select_body(sct_ref, rank_ref, v_ref, n_ref, u_ref, v0_ref, v1_ref):
    for h in range(P_HEADS):
        s0 = sct_ref[2 * h]
        s1 = sct_ref[2 * h + 1]
        _top_values(s0, v0_ref)
        _top_values(s1, v1_ref)
        top0 = v0_ref[...]
        top1 = v1_ref[...]
        best = top0[0:1, :] + top1[0:1, :]
        lo_levels = [top0[0:SUBLANES, :] + top1[j:j + 1, :] for j in range(P_TOPK)]
        hi_level0 = top0[SUBLANES:, :] + top1[0:1, :]
        levels = [jnp.concatenate([lo_levels[0], hi_level0], axis=0)] + [
            jnp.concatenate([lv, jnp.full_like(lv, NEG_INF)], axis=0) for lv in lo_levels[1:]]
        stats = {"z": jnp.zeros_like(best), "tau": best}

        def accumulate(i, m, stats=stats, best=best):
            stats["z"] = stats["z"] + jnp.exp(m - best)
            stats["tau"] = m

        _pop_top(levels, P_TOPK, accumulate)
        tau = stats["tau"]
        n = jnp.zeros_like(s0)
        for j in range(P_TOPK):
            n = jnp.where(s0 + top1[j:j + 1, :] >= tau, float(j + 1), n)
        rank1 = jnp.full(s1.shape, NOT_RANKED, F32)
        for j in reversed(range(P_TOPK)):
            rank1 = jnp.where(s1 >= top1[j:j + 1, :], float(j + 1), rank1)
        rank_ref[h] = rank1.astype(BF16)
        v_ref[h] = jnp.exp(s1 - top1[0:1, :]).astype(BF16)
        n_ref[h] = n
        u_ref[h] = jnp.exp(s0 - top0[0:1, :]) / stats["z"]


def _select(sct):
    n = sct.shape[-1]
    tl = SEL_TL
    spec = pl.BlockSpec((P_HEADS, P_NKEYS, tl), lambda i: (0, 0, i))
    shape16 = jax.ShapeDtypeStruct((P_HEADS, P_NKEYS, n), BF16)
    shape32 = jax.ShapeDtypeStruct((P_HEADS, P_NKEYS, n), F32)
    return pl.pallas_call(
        _select_body,
        grid=(n // tl,),
        in_specs=[pl.BlockSpec((2 * P_HEADS, P_NKEYS, tl), lambda i: (0, 0, i))],
        out_specs=[spec, spec, spec, spec],
        out_shape=[shape16, shape16, shape32, shape32],
        scratch_shapes=[pltpu.VMEM((P_TOPK, tl), F32), pltpu.VMEM((P_TOPK, tl), F32)],
        compiler_params=pltpu.CompilerParams(
            dimension_semantics=("arbitrary",), vmem_limit_bytes=VMEM_LIMIT),
        name="select",
    )(sct)


PEER_T = 512
PEER_EB = 1024
PEER_NI = P_N // PEER_EB
PEER_LANES = 256
BF16_ROWS = 16


def _gate_tile(rank_ref, v_ref, n_rows, u_rows, lanes):
    groups = P_NKEYS // BF16_ROWS
    width = lanes.stop - lanes.start
    g = jnp.zeros((groups, BF16_ROWS, width), BF16)
    for h in range(P_HEADS):
        nb = jnp.broadcast_to(n_rows[h][:, lanes], (BF16_ROWS, width)).astype(BF16)
        ub = jnp.broadcast_to(u_rows[h][:, lanes], (BF16_ROWS, width)).astype(BF16)
        r = rank_ref[h, :, lanes].reshape(groups, BF16_ROWS, width)
        vv = v_ref[h, :, lanes].reshape(groups, BF16_ROWS, width)
        g = g + jnp.where(r <= nb[None], vv, jnp.zeros_like(vv)) * ub[None]
    return g.reshape(P_NKEYS, width)


def _peer_body(xnt_ref, rank_ref, v_ref, n_ref, u_ref, pu_ref, pvt_ref, x2_ref, gf_ref,
               y_ref, acc_ref, w_ref, *, n_blocks):
    s = pl.program_id(0)
    i1 = jnp.minimum(s, n_blocks - 1) % PEER_NI
    i3 = jnp.clip(s - 1, 0, n_blocks - 1) % PEER_NI
    cur = s % 2
    prev = 1 - cur

    @pl.when(s == 0)
    def _():
        w_ref[...] = jnp.zeros_like(w_ref)

    @pl.when(i3 == 0)
    def _():
        acc_ref[...] = jnp.zeros_like(acc_ref)

    act = jnp.dot(pu_ref[...], xnt_ref[...], preferred_element_type=F32)
    acc_ref[...] += jnp.dot(pvt_ref[0], w_ref[prev], preferred_element_type=F32)

    for al in range(PEER_EB // P_NKEYS):
        a = i1 * (PEER_EB // P_NKEYS) + al
        rows = slice(al * P_NKEYS, (al + 1) * P_NKEYS)
        gelu = _gelu_tanh(act[rows, :].astype(BF16))
        n_rows = [n_ref[h, pl.ds(a, 1), :] for h in range(P_HEADS)]
        u_rows = [u_ref[h, pl.ds(a, 1), :] for h in range(P_HEADS)]
        for lh in range(PEER_T // PEER_LANES):
            lanes = slice(lh * PEER_LANES, (lh + 1) * PEER_LANES)
            w_ref[cur, rows, lanes] = (
                _gate_tile(rank_ref, v_ref, n_rows, u_rows, lanes) * gelu[:, lanes])

    @pl.when(jnp.logical_and(s >= 1, i3 == PEER_NI - 1))
    def _():
        y_ref[...] = _rms(x2_ref[...] + acc_ref[...].T, gf_ref[...])


def _peer(xnt, rank1, v, nn, u, pu, pvt3, x2, gf):
    n = x2.shape[0]
    t, eb = PEER_T, PEER_EB
    n_blocks = (n // t) * PEER_NI
    last = n_blocks - 1

    def pair(s, lag):
        return jnp.clip(s - lag, 0, last)

    def sel_spec():
        return pl.BlockSpec((P_HEADS, P_NKEYS, t), lambda s: (0, 0, pair(s, 0) // PEER_NI))

    return pl.pallas_call(
        functools.partial(_peer_body, n_blocks=n_blocks),
        grid=(n_blocks + 1,),
        in_specs=[
            pl.BlockSpec((D_MODEL, t), lambda s: (0, pair(s, 0) // PEER_NI)),
            sel_spec(), sel_spec(), sel_spec(), sel_spec(),
            pl.BlockSpec((eb, D_MODEL), lambda s: (pair(s, 0) % PEER_NI, 0)),
            pl.BlockSpec((1, D_MODEL, eb), lambda s: (pair(s, 1) % PEER_NI, 0, 0)),
            pl.BlockSpec((t, D_MODEL), lambda s: (pair(s, 1) // PEER_NI, 0)),
            pl.BlockSpec((1, D_MODEL), lambda s: (0, 0)),
        ],
        out_specs=pl.BlockSpec((t, D_MODEL), lambda s: (pair(s, 1) // PEER_NI, 0)),
        out_shape=jax.ShapeDtypeStruct((n, D_MODEL), F32),
        scratch_shapes=[pltpu.VMEM((D_MODEL, t), F32),
                        pltpu.VMEM((2, eb, t), BF16)],
        compiler_params=pltpu.CompilerParams(
            dimension_semantics=("arbitrary",), vmem_limit_bytes=VMEM_LIMIT),
        name="peer",
    )(xnt, rank1, v, nn, u, pu, pvt3, x2, gf)


def _trunk(x, p):
    b, l, d = x.shape
    n = b * l
    x2d = x.reshape(n, d)
    proj = _inproj(x2d, p["g1"], p["w_in"])
    o_f, o_b = _scan(proj.reshape(b, l, IN_COLS), p["lb_fwd"], p["lb_bwd"])
    x2, xnt, sct = _mix(x2d, proj, o_f.reshape(n, A_W), o_b.reshape(n, A_W), p["gn"], p["vg"],
                        p["w_sp"], p["b_sp_t"], p["w_oa"], p["w_ob"], p["w_out"], p["g2"],
                        p["w_pq"], p["keys"])
    rank1, v, nn, u = _select(sct)
    y = _peer(xnt, rank1, v, nn, u, p["pu"], p["pvt"], x2, p["gf"])
    return y.reshape(b, l, d)


def kernel(x_prompt, x_sample, norm1_g, w_in, lb_fwd, lb_bwd, gn_a, vnorm_g, w_sp, b_sp, w_oa, w_ob,
           w_out, norm2_g, w_pq, peer_keys, peer_u, peer_v, norm_f):
    layer = 0
    p = {
        "g1": norm1_g[layer].reshape(1, D_MODEL),
        "w_in": w_in[layer].astype(BF16),
        "lb_fwd": lb_fwd,
        "lb_bwd": lb_bwd,
        "gn": gn_a[layer].reshape(1, A_DV),
        "vg": vnorm_g[layer].reshape(1, B_W),
        "w_sp": w_sp[layer].astype(BF16),
        "b_sp_t": b_sp[layer].T,
        "w_oa": w_oa[layer].astype(BF16),
        "w_ob": w_ob[layer].astype(BF16),
        "w_out": w_out[layer].astype(BF16),
        "g2": norm2_g[layer].reshape(1, D_MODEL),
        "w_pq": w_pq[layer].astype(BF16),
        "keys": peer_keys[layer].reshape(2 * P_HEADS, P_NKEYS, P_DKEY).astype(BF16),
        "pu": peer_u[layer].astype(BF16),
        "pvt": peer_v[layer].astype(BF16).reshape(PEER_NI, PEER_EB, D_MODEL).transpose(0, 2, 1),
        "gf": norm_f.reshape(1, D_MODEL),
    }
    return (_trunk(x_prompt, p), _trunk(x_sample, p))
```

```python
import functools

import jax
import jax.numpy as jnp
from jax import lax
from jax.experimental import pallas as pl
from jax.experimental.pallas import tpu as pltpu

F32 = jnp.float32
BF16 = jnp.bfloat16

D_MODEL = 1024
A_HEADS = 4
A_DK = 128
A_DV = 128
A_W = A_HEADS * A_DK
A_CHUNK = 64
B_GROUPS = 4
B_GC = 128
B_W = B_GROUPS * B_GC
B_CHUNK = 128
P_HEADS = 8
P_NKEYS = 128
P_DKEY = 128
P_TOPK = 16
P_N = P_NKEYS * P_NKEYS
EPS = 1e-6
IN_COLS = 3 * A_W + 2 * A_W + 2 * B_W + 2 * D_MODEL
COL_Q, COL_FF, COL_FB, COL_I, COL_OG, COL_U, COL_V, COL_GA, COL_GB = 0, 4, 8, 12, 16, 20, 24, 28, 36

VMEM_LIMIT = 56 * 1024 * 1024

NT_DIMS = (((1,), (1,)), ((), ()))
TN_DIMS = (((0,), (0,)), ((), ()))


GELU_C1 = 0.7978845608028654
GELU_C2 = GELU_C1 * 0.044715


def _two_gelu_tanh(x):
    t = jnp.tanh(x * (GELU_C1 + GELU_C2 * (x * x)))
    return x + x * t


def _gelu_tanh(x):
    return 0.5 * _two_gelu_tanh(x)


def _rms(x, g):
    return x * lax.rsqrt(jnp.mean(x * x, axis=-1, keepdims=True) + EPS) * g


IN_TM = 512


def _inproj_body(x_ref, g_ref, w_ref, o_ref):
    h = _rms(x_ref[...], g_ref[...])
    o_ref[...] = jnp.dot(h.astype(BF16), w_ref[...], preferred_element_type=F32)


def _inproj(x2d, g1, w_in):
    n = x2d.shape[0]
    return pl.pallas_call(
        _inproj_body,
        grid=(n // IN_TM,),
        in_specs=[
            pl.BlockSpec((IN_TM, D_MODEL), lambda i: (i, 0)),
            pl.BlockSpec((1, D_MODEL), lambda i: (0, 0)),
            pl.BlockSpec((D_MODEL, IN_COLS), lambda i: (0, 0), pipeline_mode=pl.Buffered(1)),
        ],
        out_specs=pl.BlockSpec((IN_TM, IN_COLS), lambda i: (i, 0)),
        out_shape=jax.ShapeDtypeStruct((n, IN_COLS), F32),
        compiler_params=pltpu.CompilerParams(
            dimension_semantics=("arbitrary",), vmem_limit_bytes=VMEM_LIMIT),
        name="inproj",
    )(x2d, g1, w_in)


SCAN_LT = 512


def _first_softmax_row(lb_ref):
    lb = lb_ref[...]
    m = jnp.max(lb, axis=0, keepdims=True)
    e = jnp.exp(lb - m)
    return e[0:1, :] / jnp.sum(e, axis=0, keepdims=True)


def _scan_body(lbf_ref, lbb_ref, tri_ref, qf_ref, ff_ref, vf_ref, qb_ref, fb_ref, vb_ref,
               of_ref, ob_ref, sf_ref, sb_ref):
    @pl.when(pl.program_id(2) == 0)
    def _():
        sf_ref[...] = jnp.zeros_like(sf_ref)
        sb_ref[...] = jnp.zeros_like(sb_ref)

    n_chunks = SCAN_LT // A_CHUNK
    row = lax.broadcasted_iota(jnp.int32, (A_CHUNK, A_CHUNK), 0)
    col = lax.broadcasted_iota(jnp.int32, (A_CHUNK, A_CHUNK), 1)

    def rows(x, c):
        return x[c * A_CHUNK:(c + 1) * A_CHUNK, :]

    def prepare(q_ref, f_ref, v_ref, lb_ref, tri, last_row):
        lb = _first_softmax_row(lb_ref)
        qr = q_ref[...]
        q = qr * jax.nn.sigmoid(qr)
        f = lb + (1.0 - lb) * jax.nn.sigmoid(f_ref[...])
        logf = jnp.log(f)
        k = 1.0 - f
        hi = logf.astype(BF16)
        rem = logf - hi.astype(F32)
        mid = rem.astype(BF16)
        lo = (rem - mid.astype(F32)).astype(BF16)
        g3 = jnp.dot(tri, jnp.concatenate([hi, mid, lo], axis=-1), preferred_element_type=F32)
        g = (g3[:, :A_DK] + g3[:, A_DK:2 * A_DK]) + g3[:, 2 * A_DK:]
        q_dec = (q * jnp.exp(g)).astype(BF16)
        k_inv = (k * jnp.exp(-g)).astype(BF16)
        g_last = [rows(g, c)[last_row:last_row + 1, :] for c in range(n_chunks)]
        k_end = [(rows(k, c) * jnp.exp(g_last[c] - rows(g, c))).astype(BF16) for c in range(n_chunks)]
        return dict(q_dec=q_dec, k_inv=k_inv, k_end=k_end, v=v_ref[...].astype(BF16),
                    decay=[jnp.exp(gl) for gl in g_last])

    dirs = [
        dict(p=prepare(qf_ref, ff_ref, vf_ref, lbf_ref, tri_ref[0], A_CHUNK - 1),
             mask=col <= row, order=list(range(n_chunks)), o_ref=of_ref, s_ref=sf_ref),
        dict(p=prepare(qb_ref, fb_ref, vb_ref, lbb_ref, tri_ref[1], 0),
             mask=col >= row, order=list(reversed(range(n_chunks))), o_ref=ob_ref, s_ref=sb_ref),
    ]
    for d in dirs:
        p = d["p"]
        d["att"] = [lax.dot_general(rows(p["q_dec"], c), rows(p["k_inv"], c), NT_DIMS,
                                    preferred_element_type=F32) for c in range(n_chunks)]
        d["ds"] = [lax.dot_general(rows(p["v"], c), p["k_end"][c], TN_DIMS,
                                   preferred_element_type=F32) for c in range(n_chunks)]
    for d in dirs:
        p = d["p"]
        att = [jnp.where(d["mask"], a, 0.0).astype(BF16) for a in d["att"]]
        d["o"] = [jnp.dot(att[c], rows(p["v"], c), preferred_element_type=F32)
                  for c in range(n_chunks)]
    for d in dirs:
        p = d["p"]
        s_t = d["s_ref"][...]
        entering = {}
        for c in d["order"]:
            entering[c] = s_t.astype(BF16)
            s_t = s_t * p["decay"][c] + d["ds"][c]
        d["s_ref"][...] = s_t
        for c in range(n_chunks):
            o = d["o"][c] + lax.dot_general(rows(p["q_dec"], c), entering[c], NT_DIMS,
                                            preferred_element_type=F32)
            d["o_ref"][c * A_CHUNK:(c + 1) * A_CHUNK, :] = o


def _block_triangles():
    r = jnp.arange(SCAN_LT)[:, None]
    c = jnp.arange(SCAN_LT)[None, :]
    same = (r // A_CHUNK) == (c // A_CHUNK)
    return jnp.stack([same & (c <= r), same & (c >= r)]).astype(BF16)


def _scan(proj3, lb_fwd, lb_bwd):
    b, l, _ = proj3.shape
    nt = l // SCAN_LT
    blk = (None, SCAN_LT, A_DK)

    def fwd_spec(col0):
        return pl.BlockSpec(blk, lambda bi, h, j: (bi, j, col0 + h))

    def bwd_spec(col0):
        return pl.BlockSpec(blk, lambda bi, h, j: (bi, nt - 1 - j, col0 + h))

    lb_spec = pl.BlockSpec((lb_fwd.shape[0], A_DK), lambda bi, h, j: (0, h))
    tri_spec = pl.BlockSpec((2, SCAN_LT, SCAN_LT), lambda bi, h, j: (0, 0, 0))
    out_shape = jax.ShapeDtypeStruct((b, l, A_W), F32)
    return pl.pallas_call(
        _scan_body,
        grid=(b, A_HEADS, nt),
        in_specs=[lb_spec, lb_spec, tri_spec,
                  fwd_spec(COL_Q), fwd_spec(COL_FF), fwd_spec(COL_I),
                  bwd_spec(COL_Q), bwd_spec(COL_FB), bwd_spec(COL_I)],
        out_specs=[pl.BlockSpec(blk, lambda bi, h, j: (bi, j, h)),
                   pl.BlockSpec(blk, lambda bi, h, j: (bi, nt - 1 - j, h))],
        out_shape=[out_shape, out_shape],
        scratch_shapes=[pltpu.VMEM((A_DV, A_DK), F32), pltpu.VMEM((A_DV, A_DK), F32)],
        compiler_params=pltpu.CompilerParams(
            dimension_semantics=("arbitrary", "arbitrary", "arbitrary"),
            vmem_limit_bytes=VMEM_LIMIT),
        name="gla_scan",
    )(lb_fwd, lb_bwd, _block_triangles(), proj3, proj3, proj3, proj3, proj3, proj3)


MIX_TM = 512


def _mix_body(x_ref, of_ref, ob_ref, og_ref, u_ref, v_ref, ga0_ref, ga1_ref, gb0_ref, gb1_ref,
              gn_ref, vg_ref, wsp_ref, bsp_ref, woa_ref, wob_ref, wout_ref, g2_ref, wpq_ref,
              keys_ref, x2_ref, xnt_ref, sct_ref):
    o = of_ref[...] + ob_ref[...]
    og = og_ref[...]
    gate_a = og * jax.nn.sigmoid(og)
    heads = []
    for h in range(A_HEADS):
        oh = o[:, h * A_DV:(h + 1) * A_DV]
        heads.append(_rms(oh, gn_ref[...]))
    ya_in = jnp.concatenate(heads, axis=-1) * gate_a
    ya = jnp.dot(ya_in.astype(BF16), woa_ref[...], preferred_element_type=F32)

    u = _gelu_tanh(u_ref[...])
    vv = _rms(_gelu_tanh(v_ref[...]), vg_ref[...]).astype(BF16)
    rows = []
    for c in range(MIX_TM // B_CHUNK):
        groups = []
        for g in range(B_GROUPS):
            vg_blk = vv[c * B_CHUNK:(c + 1) * B_CHUNK, g * B_GC:(g + 1) * B_GC]
            m = jnp.dot(wsp_ref[g], vg_blk, preferred_element_type=F32)
            groups.append(m + bsp_ref[:, g:g + 1])
        rows.append(jnp.concatenate(groups, axis=-1))
    mixed = jnp.concatenate(rows, axis=0)
    yb = jnp.dot((u * mixed).astype(BF16), wob_ref[...], preferred_element_type=F32)

    ga = jnp.concatenate([ga0_ref[...], ga1_ref[...]], axis=-1)
    gb = jnp.concatenate([gb0_ref[...], gb1_ref[...]], axis=-1)
    merged = jax.nn.sigmoid(ga) * ya + jax.nn.sigmoid(gb) * yb
    x2 = x_ref[...] + jnp.dot(merged.astype(BF16), wout_ref[...], preferred_element_type=F32)
    x2_ref[...] = x2

    xn = _rms(x2, g2_ref[...])
    xn_bf = xn.astype(BF16)
    xnt_ref[...] = xn.T.astype(BF16)
    qh = jnp.dot(xn_bf, wpq_ref[...], preferred_element_type=F32).astype(BF16)
    for hp in range(2 * P_HEADS):
        q_hp = qh[:, hp * P_DKEY:(hp + 1) * P_DKEY]
        sct_ref[hp] = lax.dot_general(keys_ref[hp], q_hp, NT_DIMS, preferred_element_type=F32)


def _mix(x2d, proj, o_f, o_b, gn, vg, w_sp, b_sp_t, w_oa, w_ob, w_out, g2, w_pq, keys):
    n = x2d.shape[0]
    tm = MIX_TM

    def col_spec(col0):
        return pl.BlockSpec((tm, 512), lambda i: (i, col0 // 4))

    def full(a):
        return pl.BlockSpec(a.shape, lambda i: (0,) * a.ndim)

    return pl.pallas_call(
        _mix_body,
        grid=(n // tm,),
        in_specs=[
            pl.BlockSpec((tm, D_MODEL), lambda i: (i, 0)),
            pl.BlockSpec((tm, A_W), lambda i: (i, 0)),
            pl.BlockSpec((tm, A_W), lambda i: (i, 0)),
            col_spec(COL_OG), col_spec(COL_U), col_spec(COL_V),
            col_spec(COL_GA), col_spec(COL_GA + 4), col_spec(COL_GB), col_spec(COL_GB + 4),
            full(gn), full(vg), full(w_sp), full(b_sp_t), full(w_oa), full(w_ob), full(w_out),
            full(g2), full(w_pq), full(keys),
        ],
        out_specs=[
            pl.BlockSpec((tm, D_MODEL), lambda i: (i, 0)),
            pl.BlockSpec((D_MODEL, tm), lambda i: (0, i)),
            pl.BlockSpec((2 * P_HEADS, P_NKEYS, tm), lambda i: (0, 0, i)),
        ],
        out_shape=[
            jax.ShapeDtypeStruct((n, D_MODEL), F32),
            jax.ShapeDtypeStruct((D_MODEL, n), BF16),
            jax.ShapeDtypeStruct((2 * P_HEADS, P_NKEYS, n), F32),
        ],
        compiler_params=pltpu.CompilerParams(
            dimension_semantics=("arbitrary",), vmem_limit_bytes=VMEM_LIMIT),
        name="mix",
    )(x2d, o_f, o_b, proj, proj, proj, proj, proj, proj, proj,
      gn, vg, w_sp, b_sp_t, w_oa, w_ob, w_out, g2, w_pq, keys)


SEL_TL = 128
NEG_INF = float("-inf")
NOT_RANKED = float(P_TOPK + 1)
SUBLANES = 8


def _sorting_network(n):
    pairs = []
    p = 1
    while p < n:
        k = p
        while k >= 1:
            for j in range(k % p, n - k, 2 * k):
                for i in range(min(k, n - j - k)):
                    if (i + j) // (2 * p) == (i + j + k) // (2 * p):
                        pairs.append((i + j, i + j + k))
            k //= 2
        p *= 2
    return pairs


def _pop_top(levels, n_out, on_value):
    levels = list(levels)
    for i in range(n_out):
        m = jnp.max(levels[0], axis=0, keepdims=True)
        on_value(i, m)
        hit = levels[0] == m
        for k in range(min(len(levels), n_out - 1 - i)):
            below = levels[k + 1] if k + 1 < len(levels) else NEG_INF
            levels[k] = jnp.where(hit, below, levels[k])


def _top_values(s, vals_ref):
    cols = [s[k * SUBLANES:(k + 1) * SUBLANES, :] for k in range(P_NKEYS // SUBLANES)]
    for i, j in _sorting_network(len(cols)):
        cols[i], cols[j] = jnp.maximum(cols[i], cols[j]), jnp.minimum(cols[i], cols[j])

    def store(i, m):
        vals_ref[i:i + 1, :] = m

    _pop_top(cols, P_TOPK, store)


def _select_body(sct_ref, rank_ref, v_ref, n_ref, u_ref, v0_ref, v1_ref):
    for h in range(P_HEADS):
        s0 = sct_ref[2 * h]
        s1 = sct_ref[2 * h + 1]
        _top_values(s0, v0_ref)
        _top_values(s1, v1_ref)
        top0 = v0_ref[...]
        top1 = v1_ref[...]
        best = top0[0:1, :] + top1[0:1, :]
        lo_levels = [top0[0:SUBLANES, :] + top1[j:j + 1, :] for j in range(P_TOPK)]
        hi_level0 = top0[SUBLANES:, :] + top1[0:1, :]
        levels = [jnp.concatenate([lo_levels[0], hi_level0], axis=0)] + [
            jnp.concatenate([lv, jnp.full_like(lv, NEG_INF)], axis=0) for lv in lo_levels[1:]]
        stats = {"z": jnp.zeros_like(best), "tau": best}

        def accumulate(i, m, stats=stats, best=best):
            stats["z"] = stats["z"] + jnp.exp(m - best)
            stats["tau"] = m

        _pop_top(levels, P_TOPK, accumulate)
        tau = stats["tau"]
        n = jnp.zeros_like(s0)
        for j in range(P_TOPK):
            n = jnp.where(s0 + top1[j:j + 1, :] >= tau, float(j + 1), n)
        rank1 = jnp.full(s1.shape, NOT_RANKED, F32)
        for j in reversed(range(P_TOPK)):
            rank1 = jnp.where(s1 >= top1[j:j + 1, :], float(j + 1), rank1)
        rank_ref[h] = rank1.astype(BF16)
        v_ref[h] = jnp.exp(s1 - top1[0:1, :]).astype(BF16)
        n_ref[h] = n
        u_ref[h] = jnp.exp(s0 - top0[0:1, :]) * (0.5 / stats["z"])


def _select(sct):
    n = sct.shape[-1]
    tl = SEL_TL
    spec = pl.BlockSpec((P_HEADS, P_NKEYS, tl), lambda i: (0, 0, i))
    shape16 = jax.ShapeDtypeStruct((P_HEADS, P_NKEYS, n), BF16)
    shape32 = jax.ShapeDtypeStruct((P_HEADS, P_NKEYS, n), F32)
    return pl.pallas_call(
        _select_body,
        grid=(n // tl,),
        in_specs=[pl.BlockSpec((2 * P_HEADS, P_NKEYS, tl), lambda i: (0, 0, i))],
        out_specs=[spec, spec, spec, spec],
        out_shape=[shape16, shape16, shape32, shape32],
        scratch_shapes=[pltpu.VMEM((P_TOPK, tl), F32), pltpu.VMEM((P_TOPK, tl), F32)],
        compiler_params=pltpu.CompilerParams(
            dimension_semantics=("arbitrary",), vmem_limit_bytes=VMEM_LIMIT),
        name="select",
    )(sct)


PEER_T = 512
PEER_EB = 1024
PEER_NI = P_N // PEER_EB
PEER_LANES = 256
BF16_ROWS = 16


def _gate_tile(rank_ref, v_ref, n_rows, u_rows, lanes):
    groups = P_NKEYS // BF16_ROWS
    width = lanes.stop - lanes.start
    g = jnp.zeros((groups, BF16_ROWS, width), BF16)
    for h in range(P_HEADS):
        nb = jnp.broadcast_to(n_rows[h][:, lanes], (BF16_ROWS, width)).astype(BF16)
        ub = jnp.broadcast_to(u_rows[h][:, lanes], (BF16_ROWS, width)).astype(BF16)
        r = rank_ref[h, :, lanes].reshape(groups, BF16_ROWS, width)
        vv = v_ref[h, :, lanes].reshape(groups, BF16_ROWS, width)
        g = g + jnp.where(r <= nb[None], vv, jnp.zeros_like(vv)) * ub[None]
    return g.reshape(P_NKEYS, width)


def _peer_body(xnt_ref, rank_ref, v_ref, n_ref, u_ref, pu_ref, pvt_ref, x2_ref, gf_ref,
               y_ref, acc_ref, w_ref, *, n_blocks):
    s = pl.program_id(0)
    i1 = jnp.minimum(s, n_blocks - 1) % PEER_NI
    i3 = jnp.clip(s - 1, 0, n_blocks - 1) % PEER_NI
    cur = s % 2
    prev = 1 - cur

    @pl.when(s == 0)
    def _():
        w_ref[...] = jnp.zeros_like(w_ref)

    @pl.when(i3 == 0)
    def _():
        acc_ref[...] = jnp.zeros_like(acc_ref)

    act = jnp.dot(pu_ref[...], xnt_ref[...], preferred_element_type=F32)
    acc_ref[...] += jnp.dot(pvt_ref[0], w_ref[prev], preferred_element_type=F32)

    for al in range(PEER_EB // P_NKEYS):
        a = i1 * (PEER_EB // P_NKEYS) + al
        rows = slice(al * P_NKEYS, (al + 1) * P_NKEYS)
        gelu = _two_gelu_tanh(act[rows, :].astype(BF16))
        n_rows = [n_ref[h, pl.ds(a, 1), :] for h in range(P_HEADS)]
        u_rows = [u_ref[h, pl.ds(a, 1), :] for h in range(P_HEADS)]
        for lh in range(PEER_T // PEER_LANES):
            lanes = slice(lh * PEER_LANES, (lh + 1) * PEER_LANES)
            w_ref[cur, rows, lanes] = (
                _gate_tile(rank_ref, v_ref, n_rows, u_rows, lanes) * gelu[:, lanes])

    @pl.when(jnp.logical_and(s >= 1, i3 == PEER_NI - 1))
    def _():
        y_ref[...] = _rms(x2_ref[...] + acc_ref[...].T, gf_ref[...])


def _peer(xnt, rank1, v, nn, u, pu, pvt3, x2, gf):
    n = x2.shape[0]
    t, eb = PEER_T, PEER_EB
    n_blocks = (n // t) * PEER_NI
    last = n_blocks - 1

    def pair(s, lag):
        return jnp.clip(s - lag, 0, last)

    def sel_spec():
        return pl.BlockSpec((P_HEADS, P_NKEYS, t), lambda s: (0, 0, pair(s, 0) // PEER_NI))

    return pl.pallas_call(
        functools.partial(_peer_body, n_blocks=n_blocks),
        grid=(n_blocks + 1,),
        in_specs=[
            pl.BlockSpec((D_MODEL, t), lambda s: (0, pair(s, 0) // PEER_NI)),
            sel_spec(), sel_spec(), sel_spec(), sel_spec(),
            pl.BlockSpec((eb, D_MODEL), lambda s: (pair(s, 0) % PEER_NI, 0)),
            pl.BlockSpec((1, D_MODEL, eb), lambda s: (pair(s, 1) % PEER_NI, 0, 0)),
            pl.BlockSpec((t, D_MODEL), lambda s: (pair(s, 1) // PEER_NI, 0)),
            pl.BlockSpec((1, D_MODEL), lambda s: (0, 0)),
        ],
        out_specs=pl.BlockSpec((t, D_MODEL), lambda s: (pair(s, 1) // PEER_NI, 0)),
        out_shape=jax.ShapeDtypeStruct((n, D_MODEL), F32),
        scratch_shapes=[pltpu.VMEM((D_MODEL, t), F32),
                        pltpu.VMEM((2, eb, t), BF16)],
        compiler_params=pltpu.CompilerParams(
            dimension_semantics=("arbitrary",), vmem_limit_bytes=VMEM_LIMIT),
        name="peer",
    )(xnt, rank1, v, nn, u, pu, pvt3, x2, gf)


def _trunk(x, p):
    b, l, d = x.shape
    n = b * l
    x2d = x.reshape(n, d)
    proj = _inproj(x2d, p["g1"], p["w_in"])
    o_f, o_b = _scan(proj.reshape(b, l, IN_COLS), p["lb_fwd"], p["lb_bwd"])
    x2, xnt, sct = _mix(x2d, proj, o_f.reshape(n, A_W), o_b.reshape(n, A_W), p["gn"], p["vg"],
                        p["w_sp"], p["b_sp_t"], p["w_oa"], p["w_ob"], p["w_out"], p["g2"],
                        p["w_pq"], p["keys"])
    rank1, v, nn, u = _select(sct)
    y = _peer(xnt, rank1, v, nn, u, p["pu"], p["pvt"], x2, p["gf"])
    return y.reshape(b, l, d)


def kernel(x_prompt, x_sample, norm1_g, w_in, lb_fwd, lb_bwd, gn_a, vnorm_g, w_sp, b_sp, w_oa, w_ob,
           w_out, norm2_g, w_pq, peer_keys, peer_u, peer_v, norm_f):
    layer = 0
    p = {
        "g1": norm1_g[layer].reshape(1, D_MODEL),
        "w_in": w_in[layer].astype(BF16),
        "lb_fwd": lb_fwd,
        "lb_bwd": lb_bwd,
        "gn": gn_a[layer].reshape(1, A_DV),
        "vg": vnorm_g[layer].reshape(1, B_W),
        "w_sp": w_sp[layer].astype(BF16),
        "b_sp_t": b_sp[layer].T,
        "w_oa": w_oa[layer].astype(BF16),
        "w_ob": w_ob[layer].astype(BF16),
        "w_out": w_out[layer].astype(BF16),
        "g2": norm2_g[layer].reshape(1, D_MODEL),
        "w_pq": w_pq[layer].astype(BF16),
        "keys": peer_keys[layer].reshape(2 * P_HEADS, P_NKEYS, P_DKEY).astype(BF16),
        "pu": peer_u[layer].astype(BF16),
        "pvt": peer_v[layer].astype(BF16).reshape(PEER_NI, PEER_EB, D_MODEL).transpose(0, 2, 1),
        "gf": norm_f.reshape(1, D_MODEL),
    }
    return (_trunk(x_prompt, p), _trunk(x_sample, p))
```

```python
import functools

import jax
import jax.numpy as jnp
from jax import lax
from jax.experimental import pallas as pl
from jax.experimental.pallas import tpu as pltpu

F32 = jnp.float32
BF16 = jnp.bfloat16

D_MODEL = 1024
A_HEADS = 4
A_DK = 128
A_DV = 128
A_W = A_HEADS * A_DK
A_CHUNK = 64
B_GROUPS = 4
B_GC = 128
B_W = B_GROUPS * B_GC
B_CHUNK = 128
P_HEADS = 8
P_NKEYS = 128
P_DKEY = 128
P_TOPK = 16
P_N = P_NKEYS * P_NKEYS
EPS = 1e-6
IN_COLS = 3 * A_W + 2 * A_W + 2 * B_W + 2 * D_MODEL
COL_Q, COL_FF, COL_FB, COL_I, COL_OG, COL_U, COL_V, COL_GA, COL_GB = 0, 4, 8, 12, 16, 20, 24, 28, 36

VMEM_LIMIT = 56 * 1024 * 1024

NT_DIMS = (((1,), (1,)), ((), ()))
TN_DIMS = (((0,), (0,)), ((), ()))


GELU_C1 = 0.7978845608028654
GELU_C2 = GELU_C1 * 0.044715


def _two_gelu_tanh(x):
    t = jnp.tanh(x * (GELU_C1 + GELU_C2 * (x * x)))
    return x + x * t


def _gelu_tanh(x):
    return 0.5 * _two_gelu_tanh(x)


def _rms(x, g):
    return x * lax.rsqrt(jnp.mean(x * x, axis=-1, keepdims=True) + EPS) * g


IN_TM = 512


def _inproj_body(x_ref, g_ref, w_ref, o_ref):
    h = _rms(x_ref[...], g_ref[...])
    o_ref[...] = jnp.dot(h.astype(BF16), w_ref[...], preferred_element_type=F32)


def _inproj(x2d, g1, w_in):
    n = x2d.shape[0]
    return pl.pallas_call(
        _inproj_body,
        grid=(n // IN_TM,),
        in_specs=[
            pl.BlockSpec((IN_TM, D_MODEL), lambda i: (i, 0)),
            pl.BlockSpec((1, D_MODEL), lambda i: (0, 0)),
            pl.BlockSpec((D_MODEL, IN_COLS), lambda i: (0, 0), pipeline_mode=pl.Buffered(1)),
        ],
        out_specs=pl.BlockSpec((IN_TM, IN_COLS), lambda i: (i, 0)),
        out_shape=jax.ShapeDtypeStruct((n, IN_COLS), F32),
        compiler_params=pltpu.CompilerParams(
            dimension_semantics=("arbitrary",), vmem_limit_bytes=VMEM_LIMIT),
        name="inproj",
    )(x2d, g1, w_in)


SCAN_LT = 512


def _first_softmax_row(lb_ref):
    lb = lb_ref[...]
    m = jnp.max(lb, axis=0, keepdims=True)
    e = jnp.exp(lb - m)
    return e[0:1, :] / jnp.sum(e, axis=0, keepdims=True)


def _scan_body(lbf_ref, lbb_ref, tri_ref, qf_ref, ff_ref, vf_ref, qb_ref, fb_ref, vb_ref,
               of_ref, ob_ref, sf_ref, sb_ref):
    @pl.when(pl.program_id(2) == 0)
    def _():
        sf_ref[...] = jnp.zeros_like(sf_ref)
        sb_ref[...] = jnp.zeros_like(sb_ref)

    n_chunks = SCAN_LT // A_CHUNK
    row = lax.broadcasted_iota(jnp.int32, (A_CHUNK, A_CHUNK), 0)
    col = lax.broadcasted_iota(jnp.int32, (A_CHUNK, A_CHUNK), 1)

    def rows(x, c):
        return x[c * A_CHUNK:(c + 1) * A_CHUNK, :]

    def prepare(q_ref, f_ref, v_ref, lb_ref, tri, last_row):
        lb = _first_softmax_row(lb_ref)
        qr = q_ref[...]
        q = qr * jax.nn.sigmoid(qr)
        f = lb + (1.0 - lb) * jax.nn.sigmoid(f_ref[...])
        logf = jnp.log(f)
        k = 1.0 - f
        hi = logf.astype(BF16)
        rem = logf - hi.astype(F32)
        mid = rem.astype(BF16)
        lo = (rem - mid.astype(F32)).astype(BF16)
        g3 = jnp.dot(tri, jnp.concatenate([hi, mid, lo], axis=-1), preferred_element_type=F32)
        g = (g3[:, :A_DK] + g3[:, A_DK:2 * A_DK]) + g3[:, 2 * A_DK:]
        q_dec = (q * jnp.exp(g)).astype(BF16)
        k_inv = (k * jnp.exp(-g)).astype(BF16)
        g_last = [rows(g, c)[last_row:last_row + 1, :] for c in range(n_chunks)]
        k_end = [(rows(k, c) * jnp.exp(g_last[c] - rows(g, c))).astype(BF16) for c in range(n_chunks)]
        return dict(q_dec=q_dec, k_inv=k_inv, k_end=k_end, v=v_ref[...].astype(BF16),
                    decay=[jnp.exp(gl) for gl in g_last])

    dirs = [
        dict(p=prepare(qf_ref, ff_ref, vf_ref, lbf_ref, tri_ref[0], A_CHUNK - 1),
             mask=col <= row, order=list(range(n_chunks)), o_ref=of_ref, s_ref=sf_ref),
        dict(p=prepare(qb_ref, fb_ref, vb_ref, lbb_ref, tri_ref[1], 0),
             mask=col >= row, order=list(reversed(range(n_chunks))), o_ref=ob_ref, s_ref=sb_ref),
    ]
    for d in dirs:
        p = d["p"]
        d["att"] = [lax.dot_general(rows(p["q_dec"], c), rows(p["k_inv"], c), NT_DIMS,
                                    preferred_element_type=F32) for c in range(n_chunks)]
        d["ds"] = [lax.dot_general(rows(p["v"], c), p["k_end"][c], TN_DIMS,
                                   preferred_element_type=F32) for c in range(n_chunks)]
    for d in dirs:
        p = d["p"]
        att = [jnp.where(d["mask"], a, 0.0).astype(BF16) for a in d["att"]]
        d["o"] = [jnp.dot(att[c], rows(p["v"], c), preferred_element_type=F32)
                  for c in range(n_chunks)]
    for d in dirs:
        p = d["p"]
        s_t = d["s_ref"][...]
        entering = {}
        for c in d["order"]:
            entering[c] = s_t.astype(BF16)
            s_t = s_t * p["decay"][c] + d["ds"][c]
        d["s_ref"][...] = s_t
        for c in range(n_chunks):
            o = d["o"][c] + lax.dot_general(rows(p["q_dec"], c), entering[c], NT_DIMS,
                                            preferred_element_type=F32)
            d["o_ref"][c * A_CHUNK:(c + 1) * A_CHUNK, :] = o


def _block_triangles():
    r = jnp.arange(SCAN_LT)[:, None]
    c = jnp.arange(SCAN_LT)[None, :]
    same = (r // A_CHUNK) == (c // A_CHUNK)
    return jnp.stack([same & (c <= r), same & (c >= r)]).astype(BF16)


def _scan(proj3, lb_fwd, lb_bwd):
    b, l, _ = proj3.shape
    nt = l // SCAN_LT
    blk = (None, SCAN_LT, A_DK)

    def fwd_spec(col0):
        return pl.BlockSpec(blk, lambda bi, h, j: (bi, j, col0 + h))

    def bwd_spec(col0):
        return pl.BlockSpec(blk, lambda bi, h, j: (bi, nt - 1 - j, col0 + h))

    lb_spec = pl.BlockSpec((lb_fwd.shape[0], A_DK), lambda bi, h, j: (0, h))
    tri_spec = pl.BlockSpec((2, SCAN_LT, SCAN_LT), lambda bi, h, j: (0, 0, 0))
    out_shape = jax.ShapeDtypeStruct((b, l, A_W), F32)
    return pl.pallas_call(
        _scan_body,
        grid=(b, A_HEADS, nt),
        in_specs=[lb_spec, lb_spec, tri_spec,
                  fwd_spec(COL_Q), fwd_spec(COL_FF), fwd_spec(COL_I),
                  bwd_spec(COL_Q), bwd_spec(COL_FB), bwd_spec(COL_I)],
        out_specs=[pl.BlockSpec(blk, lambda bi, h, j: (bi, j, h)),
                   pl.BlockSpec(blk, lambda bi, h, j: (bi, nt - 1 - j, h))],
        out_shape=[out_shape, out_shape],
        scratch_shapes=[pltpu.VMEM((A_DV, A_DK), F32), pltpu.VMEM((A_DV, A_DK), F32)],
        compiler_params=pltpu.CompilerParams(
            dimension_semantics=("arbitrary", "arbitrary", "arbitrary"),
            vmem_limit_bytes=VMEM_LIMIT),
        name="gla_scan",
    )(lb_fwd, lb_bwd, _block_triangles(), proj3, proj3, proj3, proj3, proj3, proj3)


MIX_TM = 512


def _mix_body(x_ref, of_ref, ob_ref, og_ref, u_ref, v_ref, ga0_ref, ga1_ref, gb0_ref, gb1_ref,
              gn_ref, vg_ref, wsp_ref, bsp_ref, woa_ref, wob_ref, wout_ref, g2_ref, wpq_ref,
              keys_ref, x2_ref, xnt_ref, sct_ref):
    o = of_ref[...] + ob_ref[...]
    og = og_ref[...]
    gate_a = og * jax.nn.sigmoid(og)
    heads = []
    for h in range(A_HEADS):
        oh = o[:, h * A_DV:(h + 1) * A_DV]
        heads.append(_rms(oh, gn_ref[...]))
    ya_in = jnp.concatenate(heads, axis=-1) * gate_a
    ya = jnp.dot(ya_in.astype(BF16), woa_ref[...], preferred_element_type=F32)

    u = _gelu_tanh(u_ref[...])
    vv = _rms(_gelu_tanh(v_ref[...]), vg_ref[...]).astype(BF16)
    rows = []
    for c in range(MIX_TM // B_CHUNK):
        groups = []
        for g in range(B_GROUPS):
            vg_blk = vv[c * B_CHUNK:(c + 1) * B_CHUNK, g * B_GC:(g + 1) * B_GC]
            m = jnp.dot(wsp_ref[g], vg_blk, preferred_element_type=F32)
            groups.append(m + bsp_ref[:, g:g + 1])
        rows.append(jnp.concatenate(groups, axis=-1))
    mixed = jnp.concatenate(rows, axis=0)
    yb = jnp.dot((u * mixed).astype(BF16), wob_ref[...], preferred_element_type=F32)

    ga = jnp.concatenate([ga0_ref[...], ga1_ref[...]], axis=-1)
    gb = jnp.concatenate([gb0_ref[...], gb1_ref[...]], axis=-1)
    merged = jax.nn.sigmoid(ga) * ya + jax.nn.sigmoid(gb) * yb
    x2 = x_ref[...] + jnp.dot(merged.astype(BF16), wout_ref[...], preferred_element_type=F32)
    x2_ref[...] = x2

    xn = _rms(x2, g2_ref[...])
    xn_bf = xn.astype(BF16)
    xnt_ref[...] = xn.T.astype(BF16)
    qh = jnp.dot(xn_bf, wpq_ref[...], preferred_element_type=F32).astype(BF16)
    for hp in range(2 * P_HEADS):
        q_hp = qh[:, hp * P_DKEY:(hp + 1) * P_DKEY]
        sct_ref[hp] = lax.dot_general(keys_ref[hp], q_hp, NT_DIMS, preferred_element_type=F32)


def _mix(x2d, proj, o_f, o_b, gn, vg, w_sp, b_sp_t, w_oa, w_ob, w_out, g2, w_pq, keys):
    n = x2d.shape[0]
    tm = MIX_TM

    def col_spec(col0):
        return pl.BlockSpec((tm, 512), lambda i: (i, col0 // 4))

    def full(a):
        return pl.BlockSpec(a.shape, lambda i: (0,) * a.ndim)

    return pl.pallas_call(
        _mix_body,
        grid=(n // tm,),
        in_specs=[
            pl.BlockSpec((tm, D_MODEL), lambda i: (i, 0)),
            pl.BlockSpec((tm, A_W), lambda i: (i, 0)),
            pl.BlockSpec((tm, A_W), lambda i: (i, 0)),
            col_spec(COL_OG), col_spec(COL_U), col_spec(COL_V),
            col_spec(COL_GA), col_spec(COL_GA + 4), col_spec(COL_GB), col_spec(COL_GB + 4),
            full(gn), full(vg), full(w_sp), full(b_sp_t), full(w_oa), full(w_ob), full(w_out),
            full(g2), full(w_pq), full(keys),
        ],
        out_specs=[
            pl.BlockSpec((tm, D_MODEL), lambda i: (i, 0)),
            pl.BlockSpec((D_MODEL, tm), lambda i: (0, i)),
            pl.BlockSpec((2 * P_HEADS, P_NKEYS, tm), lambda i: (0, 0, i)),
        ],
        out_shape=[
            jax.ShapeDtypeStruct((n, D_MODEL), F32),
            jax.ShapeDtypeStruct((D_MODEL, n), BF16),
            jax.ShapeDtypeStruct((2 * P_HEADS, P_NKEYS, n), F32),
        ],
        compiler_params=pltpu.CompilerParams(
            dimension_semantics=("arbitrary",), vmem_limit_bytes=VMEM_LIMIT),
        name="mix",
    )(x2d, o_f, o_b, proj, proj, proj, proj, proj, proj, proj,
      gn, vg, w_sp, b_sp_t, w_oa, w_ob, w_out, g2, w_pq, keys)


SEL_TL = 128
NEG_INF = float("-inf")
NOT_RANKED = float(P_TOPK + 1)
SUBLANES = 8


def _sorting_network(n):
    pairs = []
    p = 1
    while p < n:
        k = p
        while k >= 1:
            for j in range(k % p, n - k, 2 * k):
                for i in range(min(k, n - j - k)):
                    if (i + j) // (2 * p) == (i + j + k) // (2 * p):
                        pairs.append((i + j, i + j + k))
            k //= 2
        p *= 2
    return pairs


def _pop_top(levels, n_out, on_value):
    levels = list(levels)
    for i in range(n_out):
        m = jnp.max(levels[0], axis=0, keepdims=True)
        on_value(i, m)
        hit = levels[0] == m
        for k in range(min(len(levels), n_out - 1 - i)):
            below = levels[k + 1] if k + 1 < len(levels) else NEG_INF
            levels[k] = jnp.where(hit, below, levels[k])


def _top_values(s, vals_ref):
    cols = [s[k * SUBLANES:(k + 1) * SUBLANES, :] for k in range(P_NKEYS // SUBLANES)]
    for i, j in _sorting_network(len(cols)):
        cols[i], cols[j] = jnp.maximum(cols[i], cols[j]), jnp.minimum(cols[i], cols[j])

    def store(i, m):
        vals_ref[i:i + 1, :] = m

    _pop_top(cols, P_TOPK, store)


def _select_body(sct_ref, rank_ref, v_ref, n_ref, u_ref, v0_ref, v1_ref):
    for h in range(P_HEADS):
        s0 = sct_ref[2 * h]
        s1 = sct_ref[2 * h + 1]
        _top_values(s0, v0_ref)
        _top_values(s1, v1_ref)
        top0 = v0_ref[...]
        top1 = v1_ref[...]
        best = top0[0:1, :] + top1[0:1, :]
        lo_levels = [top0[0:SUBLANES, :] + top1[j:j + 1, :] for j in range(P_TOPK)]
        hi_level0 = top0[SUBLANES:, :] + top1[0:1, :]
        levels = [jnp.concatenate([lo_levels[0], hi_level0], axis=0)] + [
            jnp.concatenate([lv, jnp.full_like(lv, NEG_INF)], axis=0) for lv in lo_levels[1:]]
        stats = {"z": jnp.zeros_like(best), "tau": best}

        def accumulate(i, m, stats=stats, best=best):
            stats["z"] = stats["z"] + jnp.exp(m - best)
            stats["tau"] = m

        _pop_top(levels, P_TOPK, accumulate)
        tau = stats["tau"]
        n = jnp.zeros_like(s0)
        for j in range(P_TOPK):
            n = jnp.where(s0 + top1[j:j + 1, :] >= tau, float(j + 1), n)
        rank1 = jnp.full(s1.shape, NOT_RANKED, F32)
        for j in reversed(range(P_TOPK)):
            rank1 = jnp.where(s1 >= top1[j:j + 1, :], float(j + 1), rank1)
        rank_ref[h] = rank1.astype(BF16)
        v_ref[h] = jnp.exp(s1 - top1[0:1, :]).astype(BF16)
        n_ref[h] = n
        u_ref[h] = jnp.exp(s0 - top0[0:1, :]) * (0.5 / stats["z"])


def _select(sct):
    n = sct.shape[-1]
    tl = SEL_TL
    spec = pl.BlockSpec((P_HEADS, P_NKEYS, tl), lambda i: (0, 0, i))
    shape16 = jax.ShapeDtypeStruct((P_HEADS, P_NKEYS, n), BF16)
    shape32 = jax.ShapeDtypeStruct((P_HEADS, P_NKEYS, n), F32)
    return pl.pallas_call(
        _select_body,
        grid=(n // tl,),
        in_specs=[pl.BlockSpec((2 * P_HEADS, P_NKEYS, tl), lambda i: (0, 0, i))],
        out_specs=[spec, spec, spec, spec],
        out_shape=[shape16, shape16, shape32, shape32],
        scratch_shapes=[pltpu.VMEM((P_TOPK, tl), F32), pltpu.VMEM((P_TOPK, tl), F32)],
        compiler_params=pltpu.CompilerParams(
            dimension_semantics=("arbitrary",), vmem_limit_bytes=VMEM_LIMIT),
        name="select",
    )(sct)


PEER_T = 512
PEER_EB = 2048
PEER_NI = P_N // PEER_EB
PEER_LANES = 256
BF16_ROWS = 16


def _gate_tile(rank_ref, v_ref, n_rows, u_rows, lanes):
    groups = P_NKEYS // BF16_ROWS
    width = lanes.stop - lanes.start
    g = jnp.zeros((groups, BF16_ROWS, width), BF16)
    for h in range(P_HEADS):
        nb = jnp.broadcast_to(n_rows[h][:, lanes], (BF16_ROWS, width)).astype(BF16)
        ub = jnp.broadcast_to(u_rows[h][:, lanes], (BF16_ROWS, width)).astype(BF16)
        r = rank_ref[h, :, lanes].reshape(groups, BF16_ROWS, width)
        vv = v_ref[h, :, lanes].reshape(groups, BF16_ROWS, width)
        g = g + jnp.where(r <= nb[None], vv, jnp.zeros_like(vv)) * ub[None]
    return g.reshape(P_NKEYS, width)


def _peer_body(xnt_ref, rank_ref, v_ref, n_ref, u_ref, pu_ref, pvt_ref, x2_ref, gf_ref,
               y_ref, acc_ref, w_ref, *, n_blocks):
    s = pl.program_id(0)
    i1 = jnp.minimum(s, n_blocks - 1) % PEER_NI
    i3 = jnp.clip(s - 1, 0, n_blocks - 1) % PEER_NI
    cur = s % 2
    prev = 1 - cur

    @pl.when(s == 0)
    def _():
        w_ref[...] = jnp.zeros_like(w_ref)

    @pl.when(i3 == 0)
    def _():
        acc_ref[...] = jnp.zeros_like(acc_ref)

    act = jnp.dot(pu_ref[...], xnt_ref[...], preferred_element_type=F32)
    acc_ref[...] += jnp.dot(pvt_ref[0], w_ref[prev], preferred_element_type=F32)

    for al in range(PEER_EB // P_NKEYS):
        a = i1 * (PEER_EB // P_NKEYS) + al
        rows = slice(al * P_NKEYS, (al + 1) * P_NKEYS)
        gelu = _two_gelu_tanh(act[rows, :].astype(BF16))
        n_rows = [n_ref[h, pl.ds(a, 1), :] for h in range(P_HEADS)]
        u_rows = [u_ref[h, pl.ds(a, 1), :] for h in range(P_HEADS)]
        for lh in range(PEER_T // PEER_LANES):
            lanes = slice(lh * PEER_LANES, (lh + 1) * PEER_LANES)
            w_ref[cur, rows, lanes] = (
                _gate_tile(rank_ref, v_ref, n_rows, u_rows, lanes) * gelu[:, lanes])

    @pl.when(jnp.logical_and(s >= 1, i3 == PEER_NI - 1))
    def _():
        y_ref[...] = _rms(x2_ref[...] + acc_ref[...].T, gf_ref[...])


def _peer(xnt, rank1, v, nn, u, pu, pvt3, x2, gf):
    n = x2.shape[0]
    t, eb = PEER_T, PEER_EB
    n_blocks = (n // t) * PEER_NI
    last = n_blocks - 1

    def pair(s, lag):
        return jnp.clip(s - lag, 0, last)

    def sel_spec():
        return pl.BlockSpec((P_HEADS, P_NKEYS, t), lambda s: (0, 0, pair(s, 0) // PEER_NI))

    return pl.pallas_call(
        functools.partial(_peer_body, n_blocks=n_blocks),
        grid=(n_blocks + 1,),
        in_specs=[
            pl.BlockSpec((D_MODEL, t), lambda s: (0, pair(s, 0) // PEER_NI)),
            sel_spec(), sel_spec(), sel_spec(), sel_spec(),
            pl.BlockSpec((eb, D_MODEL), lambda s: (pair(s, 0) % PEER_NI, 0)),
            pl.BlockSpec((1, D_MODEL, eb), lambda s: (pair(s, 1) % PEER_NI, 0, 0)),
            pl.BlockSpec((t, D_MODEL), lambda s: (pair(s, 1) // PEER_NI, 0)),
            pl.BlockSpec((1, D_MODEL), lambda s: (0, 0)),
        ],
        out_specs=pl.BlockSpec((t, D_MODEL), lambda s: (pair(s, 1) // PEER_NI, 0)),
        out_shape=jax.ShapeDtypeStruct((n, D_MODEL), F32),
        scratch_shapes=[pltpu.VMEM((D_MODEL, t), F32),
                        pltpu.VMEM((2, eb, t), BF16)],
        compiler_params=pltpu.CompilerParams(
            dimension_semantics=("arbitrary",), vmem_limit_bytes=VMEM_LIMIT),
        name="peer",
    )(xnt, rank1, v, nn, u, pu, pvt3, x2, gf)


def _trunk(x, p):
    b, l, d = x.shape
    n = b * l
    x2d = x.reshape(n, d)
    proj = _inproj(x2d, p["g1"], p["w_in"])
    o_f, o_b = _scan(proj.reshape(b, l, IN_COLS), p["lb_fwd"], p["lb_bwd"])
    x2, xnt, sct = _mix(x2d, proj, o_f.reshape(n, A_W), o_b.reshape(n, A_W), p["gn"], p["vg"],
                        p["w_sp"], p["b_sp_t"], p["w_oa"], p["w_ob"], p["w_out"], p["g2"],
                        p["w_pq"], p["keys"])
    rank1, v, nn, u = _select(sct)
    y = _peer(xnt, rank1, v, nn, u, p["pu"], p["pvt"], x2, p["gf"])
    return y.reshape(b, l, d)


def kernel(x_prompt, x_sample, norm1_g, w_in, lb_fwd, lb_bwd, gn_a, vnorm_g, w_sp, b_sp, w_oa, w_ob,
           w_out, norm2_g, w_pq, peer_keys, peer_u, peer_v, norm_f):
    layer = 0
    p = {
        "g1": norm1_g[layer].reshape(1, D_MODEL),
        "w_in": w_in[layer].astype(BF16),
        "lb_fwd": lb_fwd,
        "lb_bwd": lb_bwd,
        "gn": gn_a[layer].reshape(1, A_DV),
        "vg": vnorm_g[layer].reshape(1, B_W),
        "w_sp": w_sp[layer].astype(BF16),
        "b_sp_t": b_sp[layer].T,
        "w_oa": w_oa[layer].astype(BF16),
        "w_ob": w_ob[layer].astype(BF16),
        "w_out": w_out[layer].astype(BF16),
        "g2": norm2_g[layer].reshape(1, D_MODEL),
        "w_pq": w_pq[layer].astype(BF16),
        "keys": peer_keys[layer].reshape(2 * P_HEADS, P_NKEYS, P_DKEY).astype(BF16),
        "pu": peer_u[layer].astype(BF16),
        "pvt": peer_v[layer].astype(BF16).reshape(PEER_NI, PEER_EB, D_MODEL).transpose(0, 2, 1),
        "gf": norm_f.reshape(1, D_MODEL),
    }
    return (_trunk(x_prompt, p), _trunk(x_sample, p))
```

```python
import functools

import jax
import jax.numpy as jnp
from jax import lax
from jax.experimental import pallas as pl
from jax.experimental.pallas import tpu as pltpu

F32 = jnp.float32
BF16 = jnp.bfloat16

D_MODEL = 1024
A_HEADS = 4
A_DK = 128
A_DV = 128
A_W = A_HEADS * A_DK
A_CHUNK = 64
B_GROUPS = 4
B_GC = 128
B_W = B_GROUPS * B_GC
B_CHUNK = 128
P_HEADS = 8
P_NKEYS = 128
P_DKEY = 128
P_TOPK = 16
P_N = P_NKEYS * P_NKEYS
EPS = 1e-6
IN_COLS = 3 * A_W + 2 * A_W + 2 * B_W + 2 * D_MODEL
DECAY_COLS = 2 * A_W
REST_COLS = IN_COLS - DECAY_COLS
COL_FF, COL_FB = 0, 4
COL_Q, COL_I, COL_OG, COL_U, COL_V, COL_GA, COL_GB = 0, 4, 8, 12, 16, 20, 28

VMEM_LIMIT = 56 * 1024 * 1024

NT_DIMS = (((1,), (1,)), ((), ()))
TN_DIMS = (((0,), (0,)), ((), ()))


GELU_C1 = 0.7978845608028654
GELU_C2 = GELU_C1 * 0.044715


def _two_gelu_tanh(x):
    t = jnp.tanh(x * (GELU_C1 + GELU_C2 * (x * x)))
    return x + x * t


def _gelu_tanh(x):
    return 0.5 * _two_gelu_tanh(x)


def _rms(x, g):
    return x * lax.rsqrt(jnp.mean(x * x, axis=-1, keepdims=True) + EPS) * g


IN_TM = 512


def _inproj_body(x_ref, g_ref, w_ref, decay_ref, rest_ref):
    h = _rms(x_ref[...], g_ref[...])
    proj = jnp.dot(h.astype(BF16), w_ref[...], preferred_element_type=F32)
    decay_ref[...] = proj[:, :DECAY_COLS]
    rest_ref[...] = proj[:, DECAY_COLS:].astype(BF16)


def _inproj(x2d, g1, w_in):
    n = x2d.shape[0]
    return pl.pallas_call(
        _inproj_body,
        grid=(n // IN_TM,),
        in_specs=[
            pl.BlockSpec((IN_TM, D_MODEL), lambda i: (i, 0)),
            pl.BlockSpec((1, D_MODEL), lambda i: (0, 0)),
            pl.BlockSpec((D_MODEL, IN_COLS), lambda i: (0, 0), pipeline_mode=pl.Buffered(1)),
        ],
        out_specs=[pl.BlockSpec((IN_TM, DECAY_COLS), lambda i: (i, 0)),
                   pl.BlockSpec((IN_TM, REST_COLS), lambda i: (i, 0))],
        out_shape=[jax.ShapeDtypeStruct((n, DECAY_COLS), F32),
                   jax.ShapeDtypeStruct((n, REST_COLS), BF16)],
        compiler_params=pltpu.CompilerParams(
            dimension_semantics=("arbitrary",), vmem_limit_bytes=VMEM_LIMIT),
        name="inproj",
    )(x2d, g1, w_in)


SCAN_LT = 512
SCAN_HEADS = 2


def _first_softmax_row(lb_ref):
    lb = lb_ref[...]
    m = jnp.max(lb, axis=0, keepdims=True)
    e = jnp.exp(lb - m)
    return e[0:1, :] / jnp.sum(e, axis=0, keepdims=True)


def _scan_body(lbf_ref, lbb_ref, tri_ref, qf_ref, ff_ref, vf_ref, qb_ref, fb_ref, vb_ref,
               of_ref, ob_ref, sf_ref, sb_ref):
    @pl.when(pl.program_id(2) == 0)
    def _():
        sf_ref[...] = jnp.zeros_like(sf_ref)
        sb_ref[...] = jnp.zeros_like(sb_ref)

    n_chunks = SCAN_LT // A_CHUNK
    row = lax.broadcasted_iota(jnp.int32, (A_CHUNK, A_CHUNK), 0)
    col = lax.broadcasted_iota(jnp.int32, (A_CHUNK, A_CHUNK), 1)

    def rows(x, c):
        return x[c * A_CHUNK:(c + 1) * A_CHUNK, :]

    def prepare(q_ref, f_ref, v_ref, lb_ref, tri, last_row, lanes):
        lb = _first_softmax_row(lb_ref)[:, lanes]
        qr = q_ref[:, lanes].astype(F32)
        q = qr * jax.nn.sigmoid(qr)
        f = lb + (1.0 - lb) * jax.nn.sigmoid(f_ref[:, lanes])
        logf = jnp.log(f)
        k = 1.0 - f
        hi = logf.astype(BF16)
        rem = logf - hi.astype(F32)
        mid = rem.astype(BF16)
        lo = (rem - mid.astype(F32)).astype(BF16)
        g3 = jnp.dot(tri, jnp.concatenate([hi, mid, lo], axis=-1), preferred_element_type=F32)
        g = (g3[:, :A_DK] + g3[:, A_DK:2 * A_DK]) + g3[:, 2 * A_DK:]
        q_dec = (q * jnp.exp(g)).astype(BF16)
        k_inv = (k * jnp.exp(-g)).astype(BF16)
        g_last = [rows(g, c)[last_row:last_row + 1, :] for c in range(n_chunks)]
        k_end = [(rows(k, c) * jnp.exp(g_last[c] - rows(g, c))).astype(BF16) for c in range(n_chunks)]
        return dict(q_dec=q_dec, k_inv=k_inv, k_end=k_end, v=v_ref[:, lanes].astype(BF16),
                    decay=[jnp.exp(gl) for gl in g_last])

    dirs = []
    for hh in range(SCAN_HEADS):
        lanes = slice(hh * A_DK, (hh + 1) * A_DK)
        dirs.append(dict(
            p=prepare(qf_ref, ff_ref, vf_ref, lbf_ref, tri_ref[0], A_CHUNK - 1, lanes),
            mask=col <= row, order=list(range(n_chunks)), o_ref=of_ref, s_ref=sf_ref,
            head=hh, lanes=lanes))
        dirs.append(dict(
            p=prepare(qb_ref, fb_ref, vb_ref, lbb_ref, tri_ref[1], 0, lanes),
            mask=col >= row, order=list(reversed(range(n_chunks))), o_ref=ob_ref, s_ref=sb_ref,
            head=hh, lanes=lanes))
    for d in dirs:
        p = d["p"]
        d["att"] = [lax.dot_general(rows(p["q_dec"], c), rows(p["k_inv"], c), NT_DIMS,
                                    preferred_element_type=F32) for c in range(n_chunks)]
        d["ds"] = [lax.dot_general(rows(p["v"], c), p["k_end"][c], TN_DIMS,
                                   preferred_element_type=F32) for c in range(n_chunks)]
    for d in dirs:
        p = d["p"]
        att = [jnp.where(d["mask"], a, 0.0).astype(BF16) for a in d["att"]]
        d["o"] = [jnp.dot(att[c], rows(p["v"], c), preferred_element_type=F32)
                  for c in range(n_chunks)]
    for d in dirs:
        p = d["p"]
        s_t = d["s_ref"][d["head"]]
        entering = {}
        for c in d["order"]:
            entering[c] = s_t.astype(BF16)
            s_t = s_t * p["decay"][c] + d["ds"][c]
        d["s_ref"][d["head"]] = s_t
        for c in range(n_chunks):
            o = d["o"][c] + lax.dot_general(rows(p["q_dec"], c), entering[c], NT_DIMS,
                                            preferred_element_type=F32)
            d["o_ref"][c * A_CHUNK:(c + 1) * A_CHUNK, d["lanes"]] = o


def _block_triangles():
    r = jnp.arange(SCAN_LT)[:, None]
    c = jnp.arange(SCAN_LT)[None, :]
    same = (r // A_CHUNK) == (c // A_CHUNK)
    return jnp.stack([same & (c <= r), same & (c >= r)]).astype(BF16)


def _scan(decay3, rest3, lb_fwd, lb_bwd):
    b, l, _ = decay3.shape
    nt = l // SCAN_LT
    width = SCAN_HEADS * A_DK
    blk = (None, SCAN_LT, width)

    def fwd_spec(col0):
        return pl.BlockSpec(blk, lambda bi, h, j: (bi, j, col0 // SCAN_HEADS + h))

    def bwd_spec(col0):
        return pl.BlockSpec(blk, lambda bi, h, j: (bi, nt - 1 - j, col0 // SCAN_HEADS + h))

    lb_spec = pl.BlockSpec((lb_fwd.shape[0], width), lambda bi, h, j: (0, h))
    tri_spec = pl.BlockSpec((2, SCAN_LT, SCAN_LT), lambda bi, h, j: (0, 0, 0))
    out_shape = jax.ShapeDtypeStruct((b, l, A_W), F32)
    return pl.pallas_call(
        _scan_body,
        grid=(b, A_HEADS // SCAN_HEADS, nt),
        in_specs=[lb_spec, lb_spec, tri_spec,
                  fwd_spec(COL_Q), fwd_spec(COL_FF), fwd_spec(COL_I),
                  bwd_spec(COL_Q), bwd_spec(COL_FB), bwd_spec(COL_I)],
        out_specs=[pl.BlockSpec(blk, lambda bi, h, j: (bi, j, h)),
                   pl.BlockSpec(blk, lambda bi, h, j: (bi, nt - 1 - j, h))],
        out_shape=[out_shape, out_shape],
        scratch_shapes=[pltpu.VMEM((SCAN_HEADS, A_DV, A_DK), F32),
                        pltpu.VMEM((SCAN_HEADS, A_DV, A_DK), F32)],
        compiler_params=pltpu.CompilerParams(
            dimension_semantics=("arbitrary", "arbitrary", "arbitrary"),
            vmem_limit_bytes=VMEM_LIMIT),
        name="gla_scan",
    )(lb_fwd, lb_bwd, _block_triangles(), rest3, decay3, rest3, rest3, decay3, rest3)


MIX_TM = 512


def _mix_body(x_ref, of_ref, ob_ref, og_ref, u_ref, v_ref, ga0_ref, ga1_ref, gb0_ref, gb1_ref,
              gn_ref, vg_ref, wsp_ref, bsp_ref, woa_ref, wob_ref, wout_ref, g2_ref, wpq_ref,
              keys_ref, x2_ref, xnt_ref, sct_ref):
    o = of_ref[...] + ob_ref[...]
    og = og_ref[...].astype(F32)
    gate_a = og * jax.nn.sigmoid(og)
    heads = []
    for h in range(A_HEADS):
        oh = o[:, h * A_DV:(h + 1) * A_DV]
        heads.append(_rms(oh, gn_ref[...]))
    ya_in = jnp.concatenate(heads, axis=-1) * gate_a
    ya = jnp.dot(ya_in.astype(BF16), woa_ref[...], preferred_element_type=F32)

    u = _gelu_tanh(u_ref[...].astype(F32))
    vv = _rms(_gelu_tanh(v_ref[...].astype(F32)), vg_ref[...]).astype(BF16)
    rows = []
    for c in range(MIX_TM // B_CHUNK):
        groups = []
        for g in range(B_GROUPS):
            vg_blk = vv[c * B_CHUNK:(c + 1) * B_CHUNK, g * B_GC:(g + 1) * B_GC]
            m = jnp.dot(wsp_ref[g], vg_blk, preferred_element_type=F32)
            groups.append(m + bsp_ref[:, g:g + 1])
        rows.append(jnp.concatenate(groups, axis=-1))
    mixed = jnp.concatenate(rows, axis=0)
    yb = jnp.dot((u * mixed).astype(BF16), wob_ref[...], preferred_element_type=F32)

    ga = jnp.concatenate([ga0_ref[...], ga1_ref[...]], axis=-1).astype(F32)
    gb = jnp.concatenate([gb0_ref[...], gb1_ref[...]], axis=-1).astype(F32)
    merged = jax.nn.sigmoid(ga) * ya + jax.nn.sigmoid(gb) * yb
    x2 = x_ref[...] + jnp.dot(merged.astype(BF16), wout_ref[...], preferred_element_type=F32)
    x2_ref[...] = x2

    xn = _rms(x2, g2_ref[...])
    xn_bf = xn.astype(BF16)
    xnt_ref[...] = xn.T.astype(BF16)
    qh = jnp.dot(xn_bf, wpq_ref[...], preferred_element_type=F32).astype(BF16)
    for hp in range(2 * P_HEADS):
        q_hp = qh[:, hp * P_DKEY:(hp + 1) * P_DKEY]
        sct_ref[hp] = lax.dot_general(keys_ref[hp], q_hp, NT_DIMS, preferred_element_type=F32)


def _mix(x2d, rest, o_f, o_b, gn, vg, w_sp, b_sp_t, w_oa, w_ob, w_out, g2, w_pq, keys):
    n = x2d.shape[0]
    tm = MIX_TM

    def col_spec(col0):
        return pl.BlockSpec((tm, 512), lambda i: (i, col0 // 4))

    def full(a):
        return pl.BlockSpec(a.shape, lambda i: (0,) * a.ndim)

    return pl.pallas_call(
        _mix_body,
        grid=(n // tm,),
        in_specs=[
            pl.BlockSpec((tm, D_MODEL), lambda i: (i, 0)),
            pl.BlockSpec((tm, A_W), lambda i: (i, 0)),
            pl.BlockSpec((tm, A_W), lambda i: (i, 0)),
            col_spec(COL_OG), col_spec(COL_U), col_spec(COL_V),
            col_spec(COL_GA), col_spec(COL_GA + 4), col_spec(COL_GB), col_spec(COL_GB + 4),
            full(gn), full(vg), full(w_sp), full(b_sp_t), full(w_oa), full(w_ob), full(w_out),
            full(g2), full(w_pq), full(keys),
        ],
        out_specs=[
            pl.BlockSpec((tm, D_MODEL), lambda i: (i, 0)),
            pl.BlockSpec((D_MODEL, tm), lambda i: (0, i)),
            pl.BlockSpec((2 * P_HEADS, P_NKEYS, tm), lambda i: (0, 0, i)),
        ],
        out_shape=[
            jax.ShapeDtypeStruct((n, D_MODEL), F32),
            jax.ShapeDtypeStruct((D_MODEL, n), BF16),
            jax.ShapeDtypeStruct((2 * P_HEADS, P_NKEYS, n), F32),
        ],
        compiler_params=pltpu.CompilerParams(
            dimension_semantics=("arbitrary",), vmem_limit_bytes=VMEM_LIMIT),
        name="mix",
    )(x2d, o_f, o_b, rest, rest, rest, rest, rest, rest, rest,
      gn, vg, w_sp, b_sp_t, w_oa, w_ob, w_out, g2, w_pq, keys)


SEL_TL = 128
NEG_INF = float("-inf")
NOT_RANKED = float(P_TOPK + 1)
SUBLANES = 8


def _sorting_network(n):
    pairs = []
    p = 1
    while p < n:
        k = p
        while k >= 1:
            for j in range(k % p, n - k, 2 * k):
                for i in range(min(k, n - j - k)):
                    if (i + j) // (2 * p) == (i + j + k) // (2 * p):
                        pairs.append((i + j, i + j + k))
            k //= 2
        p *= 2
    return pairs


def _pop_top(levels, n_out, on_value):
    levels = list(levels)
    for i in range(n_out):
        m = jnp.max(levels[0], axis=0, keepdims=True)
        on_value(i, m)
        hit = levels[0] == m
        for k in range(min(len(levels), n_out - 1 - i)):
            below = levels[k + 1] if k + 1 < len(levels) else NEG_INF
            levels[k] = jnp.where(hit, below, levels[k])


def _top_values(s, vals_ref):
    cols = [s[k * SUBLANES:(k + 1) * SUBLANES, :] for k in range(P_NKEYS // SUBLANES)]
    for i, j in _sorting_network(len(cols)):
        cols[i], cols[j] = jnp.maximum(cols[i], cols[j]), jnp.minimum(cols[i], cols[j])

    def store(i, m):
        vals_ref[i:i + 1, :] = m

    _pop_top(cols, P_TOPK, store)


def _select_body(sct_ref, rank_ref, v_ref, n_ref, u_ref, v0_ref, v1_ref):
    for h in range(P_HEADS):
        s0 = sct_ref[2 * h]
        s1 = sct_ref[2 * h + 1]
        _top_values(s0, v0_ref)
        _top_values(s1, v1_ref)
        top0 = v0_ref[...]
        top1 = v1_ref[...]
        best = top0[0:1, :] + top1[0:1, :]
        lo_levels = [top0[0:SUBLANES, :] + top1[j:j + 1, :] for j in range(P_TOPK)]
        hi_level0 = top0[SUBLANES:, :] + top1[0:1, :]
        levels = [jnp.concatenate([lo_levels[0], hi_level0], axis=0)] + [
            jnp.concatenate([lv, jnp.full_like(lv, NEG_INF)], axis=0) for lv in lo_levels[1:]]
        stats = {"z": jnp.zeros_like(best), "tau": best}

        def accumulate(i, m, stats=stats, best=best):
            stats["z"] = stats["z"] + jnp.exp(m - best)
            stats["tau"] = m

        _pop_top(levels, P_TOPK, accumulate)
        tau = stats["tau"]
        n = jnp.zeros_like(s0)
        for j in range(P_TOPK):
            n = jnp.where(s0 + top1[j:j + 1, :] >= tau, float(j + 1), n)
        rank1 = jnp.full(s1.shape, NOT_RANKED, F32)
        for j in reversed(range(P_TOPK)):
            rank1 = jnp.where(s1 >= top1[j:j + 1, :], float(j + 1), rank1)
        rank_ref[h] = rank1.astype(BF16)
        v_ref[h] = jnp.exp(s1 - top1[0:1, :]).astype(BF16)
        n_ref[h] = n
        u_ref[h] = jnp.exp(s0 - top0[0:1, :]) * (0.5 / stats["z"])


def _select(sct):
    n = sct.shape[-1]
    tl = SEL_TL
    spec = pl.BlockSpec((P_HEADS, P_NKEYS, tl), lambda i: (0, 0, i))
    shape16 = jax.ShapeDtypeStruct((P_HEADS, P_NKEYS, n), BF16)
    shape32 = jax.ShapeDtypeStruct((P_HEADS, P_NKEYS, n), F32)
    return pl.pallas_call(
        _select_body,
        grid=(n // tl,),
        in_specs=[pl.BlockSpec((2 * P_HEADS, P_NKEYS, tl), lambda i: (0, 0, i))],
        out_specs=[spec, spec, spec, spec],
        out_shape=[shape16, shape16, shape32, shape32],
        scratch_shapes=[pltpu.VMEM((P_TOPK, tl), F32), pltpu.VMEM((P_TOPK, tl), F32)],
        compiler_params=pltpu.CompilerParams(
            dimension_semantics=("arbitrary",), vmem_limit_bytes=VMEM_LIMIT),
        name="select",
    )(sct)


PEER_T = 512
PEER_EB = 2048
PEER_NI = P_N // PEER_EB
PEER_LANES = 256
BF16_ROWS = 16


def _gate_tile(rank_ref, v_ref, n_rows, u_rows, lanes):
    groups = P_NKEYS // BF16_ROWS
    width = lanes.stop - lanes.start
    g = jnp.zeros((groups, BF16_ROWS, width), BF16)
    for h in range(P_HEADS):
        nb = jnp.broadcast_to(n_rows[h][:, lanes], (BF16_ROWS, width)).astype(BF16)
        ub = jnp.broadcast_to(u_rows[h][:, lanes], (BF16_ROWS, width)).astype(BF16)
        r = rank_ref[h, :, lanes].reshape(groups, BF16_ROWS, width)
        vv = v_ref[h, :, lanes].reshape(groups, BF16_ROWS, width)
        g = g + jnp.where(r <= nb[None], vv, jnp.zeros_like(vv)) * ub[None]
    return g.reshape(P_NKEYS, width)


def _peer_body(xnt_ref, rank_ref, v_ref, n_ref, u_ref, pu_ref, pvt_ref, x2_ref, gf_ref,
               y_ref, acc_ref, w_ref, *, n_blocks):
    s = pl.program_id(0)
    i1 = jnp.minimum(s, n_blocks - 1) % PEER_NI
    i3 = jnp.clip(s - 1, 0, n_blocks - 1) % PEER_NI
    cur = s % 2
    prev = 1 - cur

    @pl.when(s == 0)
    def _():
        w_ref[...] = jnp.zeros_like(w_ref)

    @pl.when(i3 == 0)
    def _():
        acc_ref[...] = jnp.zeros_like(acc_ref)

    act = jnp.dot(pu_ref[...], xnt_ref[...], preferred_element_type=F32)
    acc_ref[...] += jnp.dot(pvt_ref[0], w_ref[prev], preferred_element_type=F32)

    for al in range(PEER_EB // P_NKEYS):
        a = i1 * (PEER_EB // P_NKEYS) + al
        rows = slice(al * P_NKEYS, (al + 1) * P_NKEYS)
        gelu = _two_gelu_tanh(act[rows, :].astype(BF16))
        n_rows = [n_ref[h, pl.ds(a, 1), :] for h in range(P_HEADS)]
        u_rows = [u_ref[h, pl.ds(a, 1), :] for h in range(P_HEADS)]
        for lh in range(PEER_T // PEER_LANES):
            lanes = slice(lh * PEER_LANES, (lh + 1) * PEER_LANES)
            w_ref[cur, rows, lanes] = (
                _gate_tile(rank_ref, v_ref, n_rows, u_rows, lanes) * gelu[:, lanes])

    @pl.when(jnp.logical_and(s >= 1, i3 == PEER_NI - 1))
    def _():
        y_ref[...] = _rms(x2_ref[...] + acc_ref[...].T, gf_ref[...])


def _peer(xnt, rank1, v, nn, u, pu, pvt3, x2, gf):
    n = x2.shape[0]
    t, eb = PEER_T, PEER_EB
    n_blocks = (n // t) * PEER_NI
    last = n_blocks - 1

    def pair(s, lag):
        return jnp.clip(s - lag, 0, last)

    def sel_spec():
        return pl.BlockSpec((P_HEADS, P_NKEYS, t), lambda s: (0, 0, pair(s, 0) // PEER_NI))

    return pl.pallas_call(
        functools.partial(_peer_body, n_blocks=n_blocks),
        grid=(n_blocks + 1,),
        in_specs=[
            pl.BlockSpec((D_MODEL, t), lambda s: (0, pair(s, 0) // PEER_NI)),
            sel_spec(), sel_spec(), sel_spec(), sel_spec(),
            pl.BlockSpec((eb, D_MODEL), lambda s: (pair(s, 0) % PEER_NI, 0)),
            pl.BlockSpec((1, D_MODEL, eb), lambda s: (pair(s, 1) % PEER_NI, 0, 0)),
            pl.BlockSpec((t, D_MODEL), lambda s: (pair(s, 1) // PEER_NI, 0)),
            pl.BlockSpec((1, D_MODEL), lambda s: (0, 0)),
        ],
        out_specs=pl.BlockSpec((t, D_MODEL), lambda s: (pair(s, 1) // PEER_NI, 0)),
        out_shape=jax.ShapeDtypeStruct((n, D_MODEL), F32),
        scratch_shapes=[pltpu.VMEM((D_MODEL, t), F32),
                        pltpu.VMEM((2, eb, t), BF16)],
        compiler_params=pltpu.CompilerParams(
            dimension_semantics=("arbitrary",), vmem_limit_bytes=VMEM_LIMIT),
        name="peer",
    )(xnt, rank1, v, nn, u, pu, pvt3, x2, gf)


def _trunk(x, p):
    b, l, d = x.shape
    n = b * l
    x2d = x.reshape(n, d)
    decay, rest = _inproj(x2d, p["g1"], p["w_in"])
    o_f, o_b = _scan(decay.reshape(b, l, DECAY_COLS), rest.reshape(b, l, REST_COLS),
                     p["lb_fwd"], p["lb_bwd"])
    x2, xnt, sct = _mix(x2d, rest, o_f.reshape(n, A_W), o_b.reshape(n, A_W), p["gn"], p["vg"],
                        p["w_sp"], p["b_sp_t"], p["w_oa"], p["w_ob"], p["w_out"], p["g2"],
                        p["w_pq"], p["keys"])
    rank1, v, nn, u = _select(sct)
    y = _peer(xnt, rank1, v, nn, u, p["pu"], p["pvt"], x2, p["gf"])
    return y.reshape(b, l, d)


def kernel(x_prompt, x_sample, norm1_g, w_in, lb_fwd, lb_bwd, gn_a, vnorm_g, w_sp, b_sp, w_oa, w_ob,
           w_out, norm2_g, w_pq, peer_keys, peer_u, peer_v, norm_f):
    layer = 0
    p = {
        "g1": norm1_g[layer].reshape(1, D_MODEL),
        "w_in": jnp.concatenate([w_in[layer][:, A_W:3 * A_W], w_in[layer][:, :A_W],
                                 w_in[layer][:, 3 * A_W:]], axis=1).astype(BF16),
        "lb_fwd": lb_fwd,
        "lb_bwd": lb_bwd,
        "gn": gn_a[layer].reshape(1, A_DV),
        "vg": vnorm_g[layer].reshape(1, B_W),
        "w_sp": w_sp[layer].astype(BF16),
        "b_sp_t": b_sp[layer].T,
        "w_oa": w_oa[layer].astype(BF16),
        "w_ob": w_ob[layer].astype(BF16),
        "w_out": w_out[layer].astype(BF16),
        "g2": norm2_g[layer].reshape(1, D_MODEL),
        "w_pq": w_pq[layer].astype(BF16),
        "keys": peer_keys[layer].reshape(2 * P_HEADS, P_NKEYS, P_DKEY).astype(BF16),
        "pu": peer_u[layer].astype(BF16),
        "pvt": peer_v[layer].astype(BF16).reshape(PEER_NI, PEER_EB, D_MODEL).transpose(0, 2, 1),
        "gf": norm_f.reshape(1, D_MODEL),
    }
    return (_trunk(x_prompt, p), _trunk(x_sample, p))
```

```python
import functools

import jax
import jax.numpy as jnp
from jax import lax
from jax.experimental import pallas as pl
from jax.experimental.pallas import tpu as pltpu

F32 = jnp.float32
BF16 = jnp.bfloat16

D_MODEL = 1024
A_HEADS = 4
A_DK = 128
A_DV = 128
A_W = A_HEADS * A_DK
A_CHUNK = 64
B_GROUPS = 4
B_GC = 128
B_W = B_GROUPS * B_GC
B_CHUNK = 128
P_HEADS = 8
P_NKEYS = 128
P_DKEY = 128
P_TOPK = 16
P_N = P_NKEYS * P_NKEYS
EPS = 1e-6
IN_COLS = 3 * A_W + 2 * A_W + 2 * B_W + 2 * D_MODEL
COL_Q, COL_FF, COL_FB, COL_I, COL_OG, COL_U, COL_V, COL_GA, COL_GB = 0, 4, 8, 12, 16, 20, 24, 28, 36

VMEM_LIMIT = 56 * 1024 * 1024

NT_DIMS = (((1,), (1,)), ((), ()))
TN_DIMS = (((0,), (0,)), ((), ()))


GELU_C1 = 0.7978845608028654
GELU_C2 = GELU_C1 * 0.044715


def _two_gelu_tanh(x):
    t = jnp.tanh(x * (GELU_C1 + GELU_C2 * (x * x)))
    return x + x * t


def _gelu_tanh(x):
    return 0.5 * _two_gelu_tanh(x)


def _rms(x, g):
    return x * lax.rsqrt(jnp.mean(x * x, axis=-1, keepdims=True) + EPS) * g


IN_TM = 512


def _inproj_body(x_ref, g_ref, w_ref, o_ref):
    h = _rms(x_ref[...], g_ref[...])
    o_ref[...] = jnp.dot(h.astype(BF16), w_ref[...], preferred_element_type=F32)


def _inproj(x2d, g1, w_in):
    n = x2d.shape[0]
    return pl.pallas_call(
        _inproj_body,
        grid=(n // IN_TM,),
        in_specs=[
            pl.BlockSpec((IN_TM, D_MODEL), lambda i: (i, 0)),
            pl.BlockSpec((1, D_MODEL), lambda i: (0, 0)),
            pl.BlockSpec((D_MODEL, IN_COLS), lambda i: (0, 0), pipeline_mode=pl.Buffered(1)),
        ],
        out_specs=pl.BlockSpec((IN_TM, IN_COLS), lambda i: (i, 0)),
        out_shape=jax.ShapeDtypeStruct((n, IN_COLS), F32),
        compiler_params=pltpu.CompilerParams(
            dimension_semantics=("arbitrary",), vmem_limit_bytes=VMEM_LIMIT),
        name="inproj",
    )(x2d, g1, w_in)


SCAN_LT = 512
SCAN_HEADS = 2


def _first_softmax_row(lb_ref):
    lb = lb_ref[...]
    m = jnp.max(lb, axis=0, keepdims=True)
    e = jnp.exp(lb - m)
    return e[0:1, :] / jnp.sum(e, axis=0, keepdims=True)


def _scan_body(lbf_ref, lbb_ref, tri_ref, qf_ref, ff_ref, vf_ref, qb_ref, fb_ref, vb_ref,
               of_ref, ob_ref, sf_ref, sb_ref):
    @pl.when(pl.program_id(2) == 0)
    def _():
        sf_ref[...] = jnp.zeros_like(sf_ref)
        sb_ref[...] = jnp.zeros_like(sb_ref)

    n_chunks = SCAN_LT // A_CHUNK
    row = lax.broadcasted_iota(jnp.int32, (A_CHUNK, A_CHUNK), 0)
    col = lax.broadcasted_iota(jnp.int32, (A_CHUNK, A_CHUNK), 1)

    def rows(x, c):
        return x[c * A_CHUNK:(c + 1) * A_CHUNK, :]

    def prepare(q_ref, f_ref, v_ref, lb_ref, tri, last_row, lanes):
        lb = _first_softmax_row(lb_ref)[:, lanes]
        qr = q_ref[:, lanes]
        q = qr * jax.nn.sigmoid(qr)
        f = lb + (1.0 - lb) * jax.nn.sigmoid(f_ref[:, lanes])
        logf = jnp.log(f)
        k = 1.0 - f
        hi = logf.astype(BF16)
        rem = logf - hi.astype(F32)
        mid = rem.astype(BF16)
        lo = (rem - mid.astype(F32)).astype(BF16)
        g3 = jnp.dot(tri, jnp.concatenate([hi, mid, lo], axis=-1), preferred_element_type=F32)
        g = (g3[:, :A_DK] + g3[:, A_DK:2 * A_DK]) + g3[:, 2 * A_DK:]
        q_dec = (q * jnp.exp(g)).astype(BF16)
        k_inv = (k * jnp.exp(-g)).astype(BF16)
        g_last = [rows(g, c)[last_row:last_row + 1, :] for c in range(n_chunks)]
        k_end = [(rows(k, c) * jnp.exp(g_last[c] - rows(g, c))).astype(BF16) for c in range(n_chunks)]
        return dict(q_dec=q_dec, k_inv=k_inv, k_end=k_end, v=v_ref[:, lanes].astype(BF16),
                    decay=[jnp.exp(gl) for gl in g_last])

    dirs = []
    for hh in range(SCAN_HEADS):
        lanes = slice(hh * A_DK, (hh + 1) * A_DK)
        dirs.append(dict(
            p=prepare(qf_ref, ff_ref, vf_ref, lbf_ref, tri_ref[0], A_CHUNK - 1, lanes),
            mask=col <= row, order=list(range(n_chunks)), o_ref=of_ref, s_ref=sf_ref,
            head=hh, lanes=lanes))
        dirs.append(dict(
            p=prepare(qb_ref, fb_ref, vb_ref, lbb_ref, tri_ref[1], 0, lanes),
            mask=col >= row, order=list(reversed(range(n_chunks))), o_ref=ob_ref, s_ref=sb_ref,
            head=hh, lanes=lanes))
    for d in dirs:
        p = d["p"]
        d["att"] = [lax.dot_general(rows(p["q_dec"], c), rows(p["k_inv"], c), NT_DIMS,
                                    preferred_element_type=F32) for c in range(n_chunks)]
        d["ds"] = [lax.dot_general(rows(p["v"], c), p["k_end"][c], TN_DIMS,
                                   preferred_element_type=F32) for c in range(n_chunks)]
    for d in dirs:
        p = d["p"]
        att = [jnp.where(d["mask"], a, 0.0).astype(BF16) for a in d["att"]]
        d["o"] = [jnp.dot(att[c], rows(p["v"], c), preferred_element_type=F32)
                  for c in range(n_chunks)]
    for d in dirs:
        p = d["p"]
        s_t = d["s_ref"][d["head"]]
        entering = {}
        for c in d["order"]:
            entering[c] = s_t.astype(BF16)
            s_t = s_t * p["decay"][c] + d["ds"][c]
        d["s_ref"][d["head"]] = s_t
        for c in range(n_chunks):
            o = d["o"][c] + lax.dot_general(rows(p["q_dec"], c), entering[c], NT_DIMS,
                                            preferred_element_type=F32)
            d["o_ref"][c * A_CHUNK:(c + 1) * A_CHUNK, d["lanes"]] = o


def _block_triangles():
    r = jnp.arange(SCAN_LT)[:, None]
    c = jnp.arange(SCAN_LT)[None, :]
    same = (r // A_CHUNK) == (c // A_CHUNK)
    return jnp.stack([same & (c <= r), same & (c >= r)]).astype(BF16)


def _scan(proj3, lb_fwd, lb_bwd):
    b, l, _ = proj3.shape
    nt = l // SCAN_LT
    width = SCAN_HEADS * A_DK
    blk = (None, SCAN_LT, width)

    def fwd_spec(col0):
        return pl.BlockSpec(blk, lambda bi, h, j: (bi, j, col0 // SCAN_HEADS + h))

    def bwd_spec(col0):
        return pl.BlockSpec(blk, lambda bi, h, j: (bi, nt - 1 - j, col0 // SCAN_HEADS + h))

    lb_spec = pl.BlockSpec((lb_fwd.shape[0], width), lambda bi, h, j: (0, h))
    tri_spec = pl.BlockSpec((2, SCAN_LT, SCAN_LT), lambda bi, h, j: (0, 0, 0))
    out_shape = jax.ShapeDtypeStruct((b, l, A_W), F32)
    return pl.pallas_call(
        _scan_body,
        grid=(b, A_HEADS // SCAN_HEADS, nt),
        in_specs=[lb_spec, lb_spec, tri_spec,
                  fwd_spec(COL_Q), fwd_spec(COL_FF), fwd_spec(COL_I),
                  bwd_spec(COL_Q), bwd_spec(COL_FB), bwd_spec(COL_I)],
        out_specs=[pl.BlockSpec(blk, lambda bi, h, j: (bi, j, h)),
                   pl.BlockSpec(blk, lambda bi, h, j: (bi, nt - 1 - j, h))],
        out_shape=[out_shape, out_shape],
        scratch_shapes=[pltpu.VMEM((SCAN_HEADS, A_DV, A_DK), F32),
                        pltpu.VMEM((SCAN_HEADS, A_DV, A_DK), F32)],
        compiler_params=pltpu.CompilerParams(
            dimension_semantics=("arbitrary", "arbitrary", "arbitrary"),
            vmem_limit_bytes=VMEM_LIMIT),
        name="gla_scan",
    )(lb_fwd, lb_bwd, _block_triangles(), proj3, proj3, proj3, proj3, proj3, proj3)


MIX_TM = 512


def _mix_body(x_ref, of_ref, ob_ref, og_ref, u_ref, v_ref, ga0_ref, ga1_ref, gb0_ref, gb1_ref,
              gn_ref, vg_ref, wsp_ref, bsp_ref, woa_ref, wob_ref, wout_ref, g2_ref, wpq_ref,
              keys_ref, x2_ref, xnt_ref, sct_ref):
    o = of_ref[...] + ob_ref[...]
    og = og_ref[...]
    gate_a = og * jax.nn.sigmoid(og)
    heads = []
    for h in range(A_HEADS):
        oh = o[:, h * A_DV:(h + 1) * A_DV]
        heads.append(_rms(oh, gn_ref[...]))
    ya_in = jnp.concatenate(heads, axis=-1) * gate_a
    ya = jnp.dot(ya_in.astype(BF16), woa_ref[...], preferred_element_type=F32)

    u = _gelu_tanh(u_ref[...])
    vv = _rms(_gelu_tanh(v_ref[...]), vg_ref[...]).astype(BF16)
    rows = []
    for c in range(MIX_TM // B_CHUNK):
        groups = []
        for g in range(B_GROUPS):
            vg_blk = vv[c * B_CHUNK:(c + 1) * B_CHUNK, g * B_GC:(g + 1) * B_GC]
            m = jnp.dot(wsp_ref[g], vg_blk, preferred_element_type=F32)
            groups.append(m + bsp_ref[:, g:g + 1])
        rows.append(jnp.concatenate(groups, axis=-1))
    mixed = jnp.concatenate(rows, axis=0)
    yb = jnp.dot((u * mixed).astype(BF16), wob_ref[...], preferred_element_type=F32)

    ga = jnp.concatenate([ga0_ref[...], ga1_ref[...]], axis=-1)
    gb = jnp.concatenate([gb0_ref[...], gb1_ref[...]], axis=-1)
    merged = jax.nn.sigmoid(ga) * ya + jax.nn.sigmoid(gb) * yb
    x2 = x_ref[...] + jnp.dot(merged.astype(BF16), wout_ref[...], preferred_element_type=F32)
    x2_ref[...] = x2

    xn = _rms(x2, g2_ref[...])
    xn_bf = xn.astype(BF16)
    xnt_ref[...] = xn.T.astype(BF16)
    qh = jnp.dot(xn_bf, wpq_ref[...], preferred_element_type=F32).astype(BF16)
    for hp in range(2 * P_HEADS):
        q_hp = qh[:, hp * P_DKEY:(hp + 1) * P_DKEY]
        sct_ref[hp] = lax.dot_general(keys_ref[hp], q_hp, NT_DIMS, preferred_element_type=F32)


def _mix(x2d, proj, o_f, o_b, gn, vg, w_sp, b_sp_t, w_oa, w_ob, w_out, g2, w_pq, keys):
    n = x2d.shape[0]
    tm = MIX_TM

    def col_spec(col0):
        return pl.BlockSpec((tm, 512), lambda i: (i, col0 // 4))

    def full(a):
        return pl.BlockSpec(a.shape, lambda i: (0,) * a.ndim)

    return pl.pallas_call(
        _mix_body,
        grid=(n // tm,),
        in_specs=[
            pl.BlockSpec((tm, D_MODEL), lambda i: (i, 0)),
            pl.BlockSpec((tm, A_W), lambda i: (i, 0)),
            pl.BlockSpec((tm, A_W), lambda i: (i, 0)),
            col_spec(COL_OG), col_spec(COL_U), col_spec(COL_V),
            col_spec(COL_GA), col_spec(COL_GA + 4), col_spec(COL_GB), col_spec(COL_GB + 4),
            full(gn), full(vg), full(w_sp), full(b_sp_t), full(w_oa), full(w_ob), full(w_out),
            full(g2), full(w_pq), full(keys),
        ],
        out_specs=[
            pl.BlockSpec((tm, D_MODEL), lambda i: (i, 0)),
            pl.BlockSpec((D_MODEL, tm), lambda i: (0, i)),
            pl.BlockSpec((2 * P_HEADS, P_NKEYS, tm), lambda i: (0, 0, i)),
        ],
        out_shape=[
            jax.ShapeDtypeStruct((n, D_MODEL), F32),
            jax.ShapeDtypeStruct((D_MODEL, n), BF16),
            jax.ShapeDtypeStruct((2 * P_HEADS, P_NKEYS, n), F32),
        ],
        compiler_params=pltpu.CompilerParams(
            dimension_semantics=("arbitrary",), vmem_limit_bytes=VMEM_LIMIT),
        name="mix",
    )(x2d, o_f, o_b, proj, proj, proj, proj, proj, proj, proj,
      gn, vg, w_sp, b_sp_t, w_oa, w_ob, w_out, g2, w_pq, keys)


SEL_TL = 128
NEG_INF = float("-inf")
NOT_RANKED = float(P_TOPK + 1)
SUBLANES = 8


def _sorting_network(n):
    pairs = []
    p = 1
    while p < n:
        k = p
        while k >= 1:
            for j in range(k % p, n - k, 2 * k):
                for i in range(min(k, n - j - k)):
                    if (i + j) // (2 * p) == (i + j + k) // (2 * p):
                        pairs.append((i + j, i + j + k))
            k //= 2
        p *= 2
    return pairs


def _pop_top(levels, n_out, on_value):
    levels = list(levels)
    for i in range(n_out):
        m = jnp.max(levels[0], axis=0, keepdims=True)
        on_value(i, m)
        hit = levels[0] == m
        for k in range(min(len(levels), n_out - 1 - i)):
            below = levels[k + 1] if k + 1 < len(levels) else NEG_INF
            levels[k] = jnp.where(hit, below, levels[k])


def _top_values(s, vals_ref):
    cols = [s[k * SUBLANES:(k + 1) * SUBLANES, :] for k in range(P_NKEYS // SUBLANES)]
    for i, j in _sorting_network(len(cols)):
        cols[i], cols[j] = jnp.maximum(cols[i], cols[j]), jnp.minimum(cols[i], cols[j])

    def store(i, m):
        vals_ref[i:i + 1, :] = m

    _pop_top(cols, P_TOPK, store)


def _count_prefix(rows, holds):
    assert len(rows) == P_TOPK == 16
    t = lambda j: rows[j - 1]
    p8 = holds(t(8))
    p4 = holds(jnp.where(p8, t(12), t(4)))
    p2 = holds(jnp.where(p8, jnp.where(p4, t(14), t(10)), jnp.where(p4, t(6), t(2))))
    hi = jnp.where(p4, jnp.where(p2, t(15), t(13)), jnp.where(p2, t(11), t(9)))
    lo = jnp.where(p4, jnp.where(p2, t(7), t(5)), jnp.where(p2, t(3), t(1)))
    p1 = holds(jnp.where(p8, hi, lo))
    p16 = holds(t(16))
    one = lambda p, w: jnp.where(p, w, 0.0)
    return (one(p8, 8.0) + one(p4, 4.0)) + (one(p2, 2.0) + one(p1, 1.0)) + one(p16, 1.0)


def _select_body(sct_ref, rank_ref, v_ref, n_ref, u_ref, v0_ref, v1_ref):
    for h in range(P_HEADS):
        s0 = sct_ref[2 * h]
        s1 = sct_ref[2 * h + 1]
        _top_values(s0, v0_ref)
        _top_values(s1, v1_ref)
        top0 = v0_ref[...]
        top1 = v1_ref[...]
        best = top0[0:1, :] + top1[0:1, :]
        lo_levels = [top0[0:SUBLANES, :] + top1[j:j + 1, :] for j in range(P_TOPK)]
        hi_level0 = top0[SUBLANES:, :] + top1[0:1, :]
        levels = [jnp.concatenate([lo_levels[0], hi_level0], axis=0)] + [
            jnp.concatenate([lv, jnp.full_like(lv, NEG_INF)], axis=0) for lv in lo_levels[1:]]
        stats = {"z": jnp.zeros_like(best), "tau": best}

        def accumulate(i, m, stats=stats, best=best):
            stats["z"] = stats["z"] + jnp.exp(m - best)
            stats["tau"] = m

        _pop_top(levels, P_TOPK, accumulate)
        tau = stats["tau"]
        rows1 = [top1[j:j + 1, :] for j in range(P_TOPK)]
        n = _count_prefix(rows1, lambda t: s0 + t >= tau)
        rank1 = _count_prefix(rows1, lambda t: t > s1) + 1.0
        rank_ref[h] = rank1.astype(BF16)
        v_ref[h] = jnp.exp(s1 - top1[0:1, :]).astype(BF16)
        n_ref[h] = n
        u_ref[h] = jnp.exp(s0 - top0[0:1, :]) * (0.5 / stats["z"])


def _select(sct):
    n = sct.shape[-1]
    tl = SEL_TL
    spec = pl.BlockSpec((P_HEADS, P_NKEYS, tl), lambda i: (0, 0, i))
    shape16 = jax.ShapeDtypeStruct((P_HEADS, P_NKEYS, n), BF16)
    shape32 = jax.ShapeDtypeStruct((P_HEADS, P_NKEYS, n), F32)
    return pl.pallas_call(
        _select_body,
        grid=(n // tl,),
        in_specs=[pl.BlockSpec((2 * P_HEADS, P_NKEYS, tl), lambda i: (0, 0, i))],
        out_specs=[spec, spec, spec, spec],
        out_shape=[shape16, shape16, shape32, shape32],
        scratch_shapes=[pltpu.VMEM((P_TOPK, tl), F32), pltpu.VMEM((P_TOPK, tl), F32)],
        compiler_params=pltpu.CompilerParams(
            dimension_semantics=("arbitrary",), vmem_limit_bytes=VMEM_LIMIT),
        name="select",
    )(sct)


PEER_T = 512
PEER_EB = 2048
PEER_NI = P_N // PEER_EB
PEER_LANES = 256
BF16_ROWS = 16


def _gate_tile(rank_ref, v_ref, n_rows, u_rows, lanes):
    groups = P_NKEYS // BF16_ROWS
    width = lanes.stop - lanes.start
    g = jnp.zeros((groups, BF16_ROWS, width), BF16)
    for h in range(P_HEADS):
        nb = jnp.broadcast_to(n_rows[h][:, lanes], (BF16_ROWS, width)).astype(BF16)
        ub = jnp.broadcast_to(u_rows[h][:, lanes], (BF16_ROWS, width)).astype(BF16)
        r = rank_ref[h, :, lanes].reshape(groups, BF16_ROWS, width)
        vv = v_ref[h, :, lanes].reshape(groups, BF16_ROWS, width)
        g = g + jnp.where(r <= nb[None], vv, jnp.zeros_like(vv)) * ub[None]
    return g.reshape(P_NKEYS, width)


def _peer_body(xnt_ref, rank_ref, v_ref, n_ref, u_ref, pu_ref, pvt_ref, x2_ref, gf_ref,
               y_ref, acc_ref, w_ref, *, n_blocks):
    s = pl.program_id(0)
    i1 = jnp.minimum(s, n_blocks - 1) % PEER_NI
    i3 = jnp.clip(s - 1, 0, n_blocks - 1) % PEER_NI
    cur = s % 2
    prev = 1 - cur

    @pl.when(s == 0)
    def _():
        w_ref[...] = jnp.zeros_like(w_ref)

    @pl.when(i3 == 0)
    def _():
        acc_ref[...] = jnp.zeros_like(acc_ref)

    act = jnp.dot(pu_ref[...], xnt_ref[...], preferred_element_type=F32)
    acc_ref[...] += jnp.dot(pvt_ref[0], w_ref[prev], preferred_element_type=F32)

    for al in range(PEER_EB // P_NKEYS):
        a = i1 * (PEER_EB // P_NKEYS) + al
        rows = slice(al * P_NKEYS, (al + 1) * P_NKEYS)
        gelu = _two_gelu_tanh(act[rows, :].astype(BF16))
        n_rows = [n_ref[h, pl.ds(a, 1), :] for h in range(P_HEADS)]
        u_rows = [u_ref[h, pl.ds(a, 1), :] for h in range(P_HEADS)]
        for lh in range(PEER_T // PEER_LANES):
            lanes = slice(lh * PEER_LANES, (lh + 1) * PEER_LANES)
            w_ref[cur, rows, lanes] = (
                _gate_tile(rank_ref, v_ref, n_rows, u_rows, lanes) * gelu[:, lanes])

    @pl.when(jnp.logical_and(s >= 1, i3 == PEER_NI - 1))
    def _():
        y_ref[...] = _rms(x2_ref[...] + acc_ref[...].T, gf_ref[...])


def _peer(xnt, rank1, v, nn, u, pu, pvt3, x2, gf):
    n = x2.shape[0]
    t, eb = PEER_T, PEER_EB
    n_blocks = (n // t) * PEER_NI
    last = n_blocks - 1

    def pair(s, lag):
        return jnp.clip(s - lag, 0, last)

    def sel_spec():
        return pl.BlockSpec((P_HEADS, P_NKEYS, t), lambda s: (0, 0, pair(s, 0) // PEER_NI))

    return pl.pallas_call(
        functools.partial(_peer_body, n_blocks=n_blocks),
        grid=(n_blocks + 1,),
        in_specs=[
            pl.BlockSpec((D_MODEL, t), lambda s: (0, pair(s, 0) // PEER_NI)),
            sel_spec(), sel_spec(), sel_spec(), sel_spec(),
            pl.BlockSpec((eb, D_MODEL), lambda s: (pair(s, 0) % PEER_NI, 0)),
            pl.BlockSpec((1, D_MODEL, eb), lambda s: (pair(s, 1) % PEER_NI, 0, 0)),
            pl.BlockSpec((t, D_MODEL), lambda s: (pair(s, 1) // PEER_NI, 0)),
            pl.BlockSpec((1, D_MODEL), lambda s: (0, 0)),
        ],
        out_specs=pl.BlockSpec((t, D_MODEL), lambda s: (pair(s, 1) // PEER_NI, 0)),
        out_shape=jax.ShapeDtypeStruct((n, D_MODEL), F32),
        scratch_shapes=[pltpu.VMEM((D_MODEL, t), F32),
                        pltpu.VMEM((2, eb, t), BF16)],
        compiler_params=pltpu.CompilerParams(
            dimension_semantics=("arbitrary",), vmem_limit_bytes=VMEM_LIMIT),
        name="peer",
    )(xnt, rank1, v, nn, u, pu, pvt3, x2, gf)


def _trunk(x, p):
    b, l, d = x.shape
    n = b * l
    x2d = x.reshape(n, d)
    proj = _inproj(x2d, p["g1"], p["w_in"])
    o_f, o_b = _scan(proj.reshape(b, l, IN_COLS), p["lb_fwd"], p["lb_bwd"])
    x2, xnt, sct = _mix(x2d, proj, o_f.reshape(n, A_W), o_b.reshape(n, A_W), p["gn"], p["vg"],
                        p["w_sp"], p["b_sp_t"], p["w_oa"], p["w_ob"], p["w_out"], p["g2"],
                        p["w_pq"], p["keys"])
    rank1, v, nn, u = _select(sct)
    y = _peer(xnt, rank1, v, nn, u, p["pu"], p["pvt"], x2, p["gf"])
    return y.reshape(b, l, d)


def kernel(x_prompt, x_sample, norm1_g, w_in, lb_fwd, lb_bwd, gn_a, vnorm_g, w_sp, b_sp, w_oa, w_ob,
           w_out, norm2_g, w_pq, peer_keys, peer_u, peer_v, norm_f):
    layer = 0
    p = {
        "g1": norm1_g[layer].reshape(1, D_MODEL),
        "w_in": w_in[layer].astype(BF16),
        "lb_fwd": lb_fwd,
        "lb_bwd": lb_bwd,
        "gn": gn_a[layer].reshape(1, A_DV),
        "vg": vnorm_g[layer].reshape(1, B_W),
        "w_sp": w_sp[layer].astype(BF16),
        "b_sp_t": b_sp[layer].T,
        "w_oa": w_oa[layer].astype(BF16),
        "w_ob": w_ob[layer].astype(BF16),
        "w_out": w_out[layer].astype(BF16),
        "g2": norm2_g[layer].reshape(1, D_MODEL),
        "w_pq": w_pq[layer].astype(BF16),
        "keys": peer_keys[layer].reshape(2 * P_HEADS, P_NKEYS, P_DKEY).astype(BF16),
        "pu": peer_u[layer].astype(BF16),
        "pvt": peer_v[layer].astype(BF16).reshape(PEER_NI, PEER_EB, D_MODEL).transpose(0, 2, 1),
        "gf": norm_f.reshape(1, D_MODEL),
    }
    return (_trunk(x_prompt, p), _trunk(x_sample, p))
```

```python
import functools

import jax
import jax.numpy as jnp
from jax import lax
from jax.experimental import pallas as pl
from jax.experimental.pallas import tpu as pltpu

F32 = jnp.float32
BF16 = jnp.bfloat16

D_MODEL = 1024
A_HEADS = 4
A_DK = 128
A_DV = 128
A_W = A_HEADS * A_DK
A_CHUNK = 64
B_GROUPS = 4
B_GC = 128
B_W = B_GROUPS * B_GC
B_CHUNK = 128
P_HEADS = 8
P_NKEYS = 128
P_DKEY = 128
P_TOPK = 16
P_N = P_NKEYS * P_NKEYS
EPS = 1e-6
IN_COLS = 3 * A_W + 2 * A_W + 2 * B_W + 2 * D_MODEL
COL_Q, COL_FF, COL_FB, COL_I, COL_OG, COL_U, COL_V, COL_GA, COL_GB = 0, 4, 8, 12, 16, 20, 24, 28, 36

V7X_VMEM_BYTES = 64 * 1024 * 1024
VMEM_LIMIT = V7X_VMEM_BYTES - 8 * 1024 * 1024

NT_DIMS = (((1,), (1,)), ((), ()))
TN_DIMS = (((0,), (0,)), ((), ()))


GELU_C1 = 0.7978845608028654
GELU_C2 = GELU_C1 * 0.044715


def _two_gelu_tanh(x):
    t = jnp.tanh(x * (GELU_C1 + GELU_C2 * (x * x)))
    return x + x * t


def _gelu_tanh(x):
    return 0.5 * _two_gelu_tanh(x)


def _rms(x, g):
    return x * lax.rsqrt(jnp.mean(x * x, axis=-1, keepdims=True) + EPS) * g


IN_TM = 512


def _inproj_body(x_ref, g_ref, w_ref, o_ref):
    h = _rms(x_ref[...], g_ref[...])
    o_ref[...] = jnp.dot(h.astype(BF16), w_ref[...], preferred_element_type=F32)


def _inproj(x2d, g1, w_in):
    n = x2d.shape[0]
    return pl.pallas_call(
        _inproj_body,
        grid=(n // IN_TM,),
        in_specs=[
            pl.BlockSpec((IN_TM, D_MODEL), lambda i: (i, 0)),
            pl.BlockSpec((1, D_MODEL), lambda i: (0, 0)),
            pl.BlockSpec((D_MODEL, IN_COLS), lambda i: (0, 0), pipeline_mode=pl.Buffered(1)),
        ],
        out_specs=pl.BlockSpec((IN_TM, IN_COLS), lambda i: (i, 0)),
        out_shape=jax.ShapeDtypeStruct((n, IN_COLS), F32),
        compiler_params=pltpu.CompilerParams(
            dimension_semantics=("arbitrary",), vmem_limit_bytes=VMEM_LIMIT),
        name="inproj",
    )(x2d, g1, w_in)


SCAN_LT = 512
SCAN_HEADS = 4
TRI_ROWS = 256


def _first_softmax_row(lb_ref):
    lb = lb_ref[...]
    m = jnp.max(lb, axis=0, keepdims=True)
    e = jnp.exp(lb - m)
    return e[0:1, :] / jnp.sum(e, axis=0, keepdims=True)


def _scan_body(lbf_ref, lbb_ref, tri_ref, qf_ref, ff_ref, vf_ref, qb_ref, fb_ref, vb_ref,
               of_ref, ob_ref, sf_ref, sb_ref):
    @pl.when(pl.program_id(2) == 0)
    def _():
        sf_ref[...] = jnp.zeros_like(sf_ref)
        sb_ref[...] = jnp.zeros_like(sb_ref)

    n_chunks = SCAN_LT // A_CHUNK
    row = lax.broadcasted_iota(jnp.int32, (A_CHUNK, A_CHUNK), 0)
    col = lax.broadcasted_iota(jnp.int32, (A_CHUNK, A_CHUNK), 1)

    def rows(x, c):
        return x[c * A_CHUNK:(c + 1) * A_CHUNK, :]

    def prepare(q_ref, f_ref, v_ref, lb_ref, tri, last_row, lanes):
        lb = _first_softmax_row(lb_ref)[:, lanes]
        qr = q_ref[:, lanes]
        q = qr * jax.nn.sigmoid(qr)
        f = lb + (1.0 - lb) * jax.nn.sigmoid(f_ref[:, lanes])
        logf = jnp.log(f)
        k = 1.0 - f
        hi = logf.astype(BF16)
        rem = logf - hi.astype(F32)
        mid = rem.astype(BF16)
        lo = (rem - mid.astype(F32)).astype(BF16)
        terms = jnp.concatenate([hi, mid, lo], axis=-1)
        g3 = jnp.concatenate(
            [jnp.dot(tri, terms[i * TRI_ROWS:(i + 1) * TRI_ROWS, :], preferred_element_type=F32)
             for i in range(SCAN_LT // TRI_ROWS)], axis=0)
        g = (g3[:, :A_DK] + g3[:, A_DK:2 * A_DK]) + g3[:, 2 * A_DK:]
        q_dec = (q * jnp.exp(g)).astype(BF16)
        k_inv = (k * jnp.exp(-g)).astype(BF16)
        g_last = [rows(g, c)[last_row:last_row + 1, :] for c in range(n_chunks)]
        k_end = [(rows(k, c) * jnp.exp(g_last[c] - rows(g, c))).astype(BF16) for c in range(n_chunks)]
        return dict(q_dec=q_dec, k_inv=k_inv, k_end=k_end, v=v_ref[:, lanes].astype(BF16),
                    decay=[jnp.exp(gl) for gl in g_last])

    dirs = []
    for hh in range(SCAN_HEADS):
        lanes = slice(hh * A_DK, (hh + 1) * A_DK)
        dirs.append(dict(
            p=prepare(qf_ref, ff_ref, vf_ref, lbf_ref, tri_ref[0], A_CHUNK - 1, lanes),
            mask=col <= row, order=list(range(n_chunks)), o_ref=of_ref, s_ref=sf_ref,
            head=hh, lanes=lanes))
        dirs.append(dict(
            p=prepare(qb_ref, fb_ref, vb_ref, lbb_ref, tri_ref[1], 0, lanes),
            mask=col >= row, order=list(reversed(range(n_chunks))), o_ref=ob_ref, s_ref=sb_ref,
            head=hh, lanes=lanes))
    for d in dirs:
        p = d["p"]
        d["att"] = [lax.dot_general(rows(p["q_dec"], c), rows(p["k_inv"], c), NT_DIMS,
                                    preferred_element_type=F32) for c in range(n_chunks)]
        d["ds"] = [lax.dot_general(rows(p["v"], c), p["k_end"][c], TN_DIMS,
                                   preferred_element_type=F32) for c in range(n_chunks)]
    for d in dirs:
        p = d["p"]
        att = [jnp.where(d["mask"], a, 0.0).astype(BF16) for a in d["att"]]
        d["o"] = [jnp.dot(att[c], rows(p["v"], c), preferred_element_type=F32)
                  for c in range(n_chunks)]
    for d in dirs:
        p = d["p"]
        s_t = d["s_ref"][d["head"]]
        entering = {}
        for c in d["order"]:
            entering[c] = s_t.astype(BF16)
            s_t = s_t * p["decay"][c] + d["ds"][c]
        d["s_ref"][d["head"]] = s_t
        for c in range(n_chunks):
            o = d["o"][c] + lax.dot_general(rows(p["q_dec"], c), entering[c], NT_DIMS,
                                            preferred_element_type=F32)
            d["o_ref"][c * A_CHUNK:(c + 1) * A_CHUNK, d["lanes"]] = o


def _block_triangles():
    r = jnp.arange(TRI_ROWS)[:, None]
    c = jnp.arange(TRI_ROWS)[None, :]
    same = (r // A_CHUNK) == (c // A_CHUNK)
    return jnp.stack([same & (c <= r), same & (c >= r)]).astype(BF16)


def _scan(proj3, lb_fwd, lb_bwd):
    b, l, _ = proj3.shape
    nt = l // SCAN_LT
    width = SCAN_HEADS * A_DK
    blk = (None, SCAN_LT, width)

    def fwd_spec(col0):
        return pl.BlockSpec(blk, lambda bi, h, j: (bi, j, col0 // SCAN_HEADS + h))

    def bwd_spec(col0):
        return pl.BlockSpec(blk, lambda bi, h, j: (bi, nt - 1 - j, col0 // SCAN_HEADS + h))

    lb_spec = pl.BlockSpec((lb_fwd.shape[0], width), lambda bi, h, j: (0, h))
    tri_spec = pl.BlockSpec((2, TRI_ROWS, TRI_ROWS), lambda bi, h, j: (0, 0, 0))
    out_shape = jax.ShapeDtypeStruct((b, l, A_W), F32)
    return pl.pallas_call(
        _scan_body,
        grid=(b, A_HEADS // SCAN_HEADS, nt),
        in_specs=[lb_spec, lb_spec, tri_spec,
                  fwd_spec(COL_Q), fwd_spec(COL_FF), fwd_spec(COL_I),
                  bwd_spec(COL_Q), bwd_spec(COL_FB), bwd_spec(COL_I)],
        out_specs=[pl.BlockSpec(blk, lambda bi, h, j: (bi, j, h)),
                   pl.BlockSpec(blk, lambda bi, h, j: (bi, nt - 1 - j, h))],
        out_shape=[out_shape, out_shape],
        scratch_shapes=[pltpu.VMEM((SCAN_HEADS, A_DV, A_DK), F32),
                        pltpu.VMEM((SCAN_HEADS, A_DV, A_DK), F32)],
        compiler_params=pltpu.CompilerParams(
            dimension_semantics=("arbitrary", "arbitrary", "arbitrary"),
            vmem_limit_bytes=VMEM_LIMIT),
        name="gla_scan",
    )(lb_fwd, lb_bwd, _block_triangles(), proj3, proj3, proj3, proj3, proj3, proj3)


MIX_TM = 512


def _mix_body(x_ref, of_ref, ob_ref, og_ref, u_ref, v_ref, ga0_ref, ga1_ref, gb0_ref, gb1_ref,
              gn_ref, vg_ref, wsp_ref, bsp_ref, woa_ref, wob_ref, wout_ref, g2_ref, wpq_ref,
              keys_ref, x2_ref, xnt_ref, sct_ref):
    o = of_ref[...] + ob_ref[...]
    og = og_ref[...]
    gate_a = og * jax.nn.sigmoid(og)
    heads = []
    for h in range(A_HEADS):
        oh = o[:, h * A_DV:(h + 1) * A_DV]
        heads.append(_rms(oh, gn_ref[...]))
    ya_in = jnp.concatenate(heads, axis=-1) * gate_a
    ya = jnp.dot(ya_in.astype(BF16), woa_ref[...], preferred_element_type=F32)

    u = _gelu_tanh(u_ref[...])
    vv = _rms(_gelu_tanh(v_ref[...]), vg_ref[...]).astype(BF16)
    rows = []
    for c in range(MIX_TM // B_CHUNK):
        groups = []
        for g in range(B_GROUPS):
            vg_blk = vv[c * B_CHUNK:(c + 1) * B_CHUNK, g * B_GC:(g + 1) * B_GC]
            m = jnp.dot(wsp_ref[g], vg_blk, preferred_element_type=F32)
            groups.append(m + bsp_ref[:, g:g + 1])
        rows.append(jnp.concatenate(groups, axis=-1))
    mixed = jnp.concatenate(rows, axis=0)
    yb = jnp.dot((u * mixed).astype(BF16), wob_ref[...], preferred_element_type=F32)

    ga = jnp.concatenate([ga0_ref[...], ga1_ref[...]], axis=-1)
    gb = jnp.concatenate([gb0_ref[...], gb1_ref[...]], axis=-1)
    merged = jax.nn.sigmoid(ga) * ya + jax.nn.sigmoid(gb) * yb
    x2 = x_ref[...] + jnp.dot(merged.astype(BF16), wout_ref[...], preferred_element_type=F32)
    x2_ref[...] = x2

    xn = _rms(x2, g2_ref[...])
    xn_bf = xn.astype(BF16)
    xnt_ref[...] = xn.T.astype(BF16)
    qh = jnp.dot(xn_bf, wpq_ref[...], preferred_element_type=F32).astype(BF16)
    for hp in range(2 * P_HEADS):
        q_hp = qh[:, hp * P_DKEY:(hp + 1) * P_DKEY]
        sct_ref[hp] = lax.dot_general(keys_ref[hp], q_hp, NT_DIMS, preferred_element_type=F32)


def _mix(x2d, proj, o_f, o_b, gn, vg, w_sp, b_sp_t, w_oa, w_ob, w_out, g2, w_pq, keys):
    n = x2d.shape[0]
    tm = MIX_TM

    def col_spec(col0):
        return pl.BlockSpec((tm, 512), lambda i: (i, col0 // 4))

    def full(a):
        return pl.BlockSpec(a.shape, lambda i: (0,) * a.ndim)

    return pl.pallas_call(
        _mix_body,
        grid=(n // tm,),
        in_specs=[
            pl.BlockSpec((tm, D_MODEL), lambda i: (i, 0)),
            pl.BlockSpec((tm, A_W), lambda i: (i, 0)),
            pl.BlockSpec((tm, A_W), lambda i: (i, 0)),
            col_spec(COL_OG), col_spec(COL_U), col_spec(COL_V),
            col_spec(COL_GA), col_spec(COL_GA + 4), col_spec(COL_GB), col_spec(COL_GB + 4),
            full(gn), full(vg), full(w_sp), full(b_sp_t), full(w_oa), full(w_ob), full(w_out),
            full(g2), full(w_pq), full(keys),
        ],
        out_specs=[
            pl.BlockSpec((tm, D_MODEL), lambda i: (i, 0)),
            pl.BlockSpec((D_MODEL, tm), lambda i: (0, i)),
            pl.BlockSpec((2 * P_HEADS, P_NKEYS, tm), lambda i: (0, 0, i)),
        ],
        out_shape=[
            jax.ShapeDtypeStruct((n, D_MODEL), F32),
            jax.ShapeDtypeStruct((D_MODEL, n), BF16),
            jax.ShapeDtypeStruct((2 * P_HEADS, P_NKEYS, n), F32),
        ],
        compiler_params=pltpu.CompilerParams(
            dimension_semantics=("arbitrary",), vmem_limit_bytes=VMEM_LIMIT),
        name="mix",
    )(x2d, o_f, o_b, proj, proj, proj, proj, proj, proj, proj,
      gn, vg, w_sp, b_sp_t, w_oa, w_ob, w_out, g2, w_pq, keys)


SEL_TL = 128
NEG_INF = float("-inf")
NOT_RANKED = float(P_TOPK + 1)
SUBLANES = 8


def _sorting_network(n):
    pairs = []
    p = 1
    while p < n:
        k = p
        while k >= 1:
            for j in range(k % p, n - k, 2 * k):
                for i in range(min(k, n - j - k)):
                    if (i + j) // (2 * p) == (i + j + k) // (2 * p):
                        pairs.append((i + j, i + j + k))
            k //= 2
        p *= 2
    return pairs


def _pop_top(levels, n_out, on_value):
    levels = list(levels)
    for i in range(n_out):
        m = jnp.max(levels[0], axis=0, keepdims=True)
        on_value(i, m)
        hit = levels[0] == m
        for k in range(min(len(levels), n_out - 1 - i)):
            below = levels[k + 1] if k + 1 < len(levels) else NEG_INF
            levels[k] = jnp.where(hit, below, levels[k])


def _top_values(s, vals_ref):
    cols = [s[k * SUBLANES:(k + 1) * SUBLANES, :] for k in range(P_NKEYS // SUBLANES)]
    for i, j in _sorting_network(len(cols)):
        cols[i], cols[j] = jnp.maximum(cols[i], cols[j]), jnp.minimum(cols[i], cols[j])

    def store(i, m):
        vals_ref[i:i + 1, :] = m

    _pop_top(cols, P_TOPK, store)


def _count_prefix(rows, holds):
    assert len(rows) == P_TOPK == 16
    t = lambda j: rows[j - 1]
    p8 = holds(t(8))
    p4 = holds(jnp.where(p8, t(12), t(4)))
    p2 = holds(jnp.where(p8, jnp.where(p4, t(14), t(10)), jnp.where(p4, t(6), t(2))))
    hi = jnp.where(p4, jnp.where(p2, t(15), t(13)), jnp.where(p2, t(11), t(9)))
    lo = jnp.where(p4, jnp.where(p2, t(7), t(5)), jnp.where(p2, t(3), t(1)))
    p1 = holds(jnp.where(p8, hi, lo))
    p16 = holds(t(16))
    one = lambda p, w: jnp.where(p, w, 0.0)
    return (one(p8, 8.0) + one(p4, 4.0)) + (one(p2, 2.0) + one(p1, 1.0)) + one(p16, 1.0)


def _select_body(sct_ref, rank_ref, v_ref, n_ref, u_ref, v0_ref, v1_ref):
    for h in range(P_HEADS):
        s0 = sct_ref[2 * h]
        s1 = sct_ref[2 * h + 1]
        _top_values(s0, v0_ref)
        _top_values(s1, v1_ref)
        top0 = v0_ref[...]
        top1 = v1_ref[...]
        best = top0[0:1, :] + top1[0:1, :]
        lo_levels = [top0[0:SUBLANES, :] + top1[j:j + 1, :] for j in range(P_TOPK)]
        hi_level0 = top0[SUBLANES:, :] + top1[0:1, :]
        levels = [jnp.concatenate([lo_levels[0], hi_level0], axis=0)] + [
            jnp.concatenate([lv, jnp.full_like(lv, NEG_INF)], axis=0) for lv in lo_levels[1:]]
        stats = {"z": jnp.zeros_like(best), "tau": best}

        def accumulate(i, m, stats=stats, best=best):
            stats["z"] = stats["z"] + jnp.exp(m - best)
            stats["tau"] = m

        _pop_top(levels, P_TOPK, accumulate)
        tau = stats["tau"]
        rows1 = [top1[j:j + 1, :] for j in range(P_TOPK)]
        n = _count_prefix(rows1, lambda t: s0 + t >= tau)
        rank1 = _count_prefix(rows1, lambda t: t > s1) + 1.0
        rank_ref[h] = rank1.astype(BF16)
        v_ref[h] = jnp.exp(s1 - top1[0:1, :]).astype(BF16)
        n_ref[h] = n
        u_ref[h] = jnp.exp(s0 - top0[0:1, :]) * (0.5 / stats["z"])


def _select(sct):
    n = sct.shape[-1]
    tl = SEL_TL
    spec = pl.BlockSpec((P_HEADS, P_NKEYS, tl), lambda i: (0, 0, i))
    shape16 = jax.ShapeDtypeStruct((P_HEADS, P_NKEYS, n), BF16)
    shape32 = jax.ShapeDtypeStruct((P_HEADS, P_NKEYS, n), F32)
    return pl.pallas_call(
        _select_body,
        grid=(n // tl,),
        in_specs=[pl.BlockSpec((2 * P_HEADS, P_NKEYS, tl), lambda i: (0, 0, i))],
        out_specs=[spec, spec, spec, spec],
        out_shape=[shape16, shape16, shape32, shape32],
        scratch_shapes=[pltpu.VMEM((P_TOPK, tl), F32), pltpu.VMEM((P_TOPK, tl), F32)],
        compiler_params=pltpu.CompilerParams(
            dimension_semantics=("arbitrary",), vmem_limit_bytes=VMEM_LIMIT),
        name="select",
    )(sct)


PEER_T = 512
PEER_EB = 2048
PEER_NI = P_N // PEER_EB
PEER_LANES = 256
BF16_ROWS = 16


def _gate_tile(rank_ref, v_ref, n_rows, u_rows, lanes):
    groups = P_NKEYS // BF16_ROWS
    width = lanes.stop - lanes.start
    g = jnp.zeros((groups, BF16_ROWS, width), BF16)
    for h in range(P_HEADS):
        nb = jnp.broadcast_to(n_rows[h][:, lanes], (BF16_ROWS, width)).astype(BF16)
        ub = jnp.broadcast_to(u_rows[h][:, lanes], (BF16_ROWS, width)).astype(BF16)
        r = rank_ref[h, :, lanes].reshape(groups, BF16_ROWS, width)
        vv = v_ref[h, :, lanes].reshape(groups, BF16_ROWS, width)
        g = g + jnp.where(r <= nb[None], vv, jnp.zeros_like(vv)) * ub[None]
    return g.reshape(P_NKEYS, width)


def _peer_body(xnt_ref, rank_ref, v_ref, n_ref, u_ref, pu_ref, pvt_ref, x2_ref, gf_ref,
               y_ref, acc_ref, w_ref, *, n_blocks):
    s = pl.program_id(0)
    i1 = jnp.minimum(s, n_blocks - 1) % PEER_NI
    i3 = jnp.clip(s - 1, 0, n_blocks - 1) % PEER_NI
    cur = s % 2
    prev = 1 - cur

    @pl.when(s == 0)
    def _():
        w_ref[...] = jnp.zeros_like(w_ref)

    @pl.when(i3 == 0)
    def _():
        acc_ref[...] = jnp.zeros_like(acc_ref)

    act = jnp.dot(pu_ref[...], xnt_ref[...], preferred_element_type=F32)
    acc_ref[...] += jnp.dot(pvt_ref[0], w_ref[prev], preferred_element_type=F32)

    for al in range(PEER_EB // P_NKEYS):
        a = i1 * (PEER_EB // P_NKEYS) + al
        rows = slice(al * P_NKEYS, (al + 1) * P_NKEYS)
        gelu = _two_gelu_tanh(act[rows, :].astype(BF16))
        n_rows = [n_ref[h, pl.ds(a, 1), :] for h in range(P_HEADS)]
        u_rows = [u_ref[h, pl.ds(a, 1), :] for h in range(P_HEADS)]
        for lh in range(PEER_T // PEER_LANES):
            lanes = slice(lh * PEER_LANES, (lh + 1) * PEER_LANES)
            w_ref[cur, rows, lanes] = (
                _gate_tile(rank_ref, v_ref, n_rows, u_rows, lanes) * gelu[:, lanes])

    @pl.when(jnp.logical_and(s >= 1, i3 == PEER_NI - 1))
    def _():
        y_ref[...] = _rms(x2_ref[...] + acc_ref[...].T, gf_ref[...])


def _peer(xnt, rank1, v, nn, u, pu, pvt3, x2, gf):
    n = x2.shape[0]
    t, eb = PEER_T, PEER_EB
    n_blocks = (n // t) * PEER_NI
    last = n_blocks - 1

    def pair(s, lag):
        return jnp.clip(s - lag, 0, last)

    def sel_spec():
        return pl.BlockSpec((P_HEADS, P_NKEYS, t), lambda s: (0, 0, pair(s, 0) // PEER_NI))

    return pl.pallas_call(
        functools.partial(_peer_body, n_blocks=n_blocks),
        grid=(n_blocks + 1,),
        in_specs=[
            pl.BlockSpec((D_MODEL, t), lambda s: (0, pair(s, 0) // PEER_NI)),
            sel_spec(), sel_spec(), sel_spec(), sel_spec(),
            pl.BlockSpec((eb, D_MODEL), lambda s: (pair(s, 0) % PEER_NI, 0)),
            pl.BlockSpec((1, D_MODEL, eb), lambda s: (pair(s, 1) % PEER_NI, 0, 0)),
            pl.BlockSpec((t, D_MODEL), lambda s: (pair(s, 1) // PEER_NI, 0)),
            pl.BlockSpec((1, D_MODEL), lambda s: (0, 0)),
        ],
        out_specs=pl.BlockSpec((t, D_MODEL), lambda s: (pair(s, 1) // PEER_NI, 0)),
        out_shape=jax.ShapeDtypeStruct((n, D_MODEL), F32),
        scratch_shapes=[pltpu.VMEM((D_MODEL, t), F32),
                        pltpu.VMEM((2, eb, t), BF16)],
        compiler_params=pltpu.CompilerParams(
            dimension_semantics=("arbitrary",), vmem_limit_bytes=VMEM_LIMIT),
        name="peer",
    )(xnt, rank1, v, nn, u, pu, pvt3, x2, gf)


def _trunk(x, p):
    b, l, d = x.shape
    n = b * l
    x2d = x.reshape(n, d)
    proj = _inproj(x2d, p["g1"], p["w_in"])
    o_f, o_b = _scan(proj.reshape(b, l, IN_COLS), p["lb_fwd"], p["lb_bwd"])
    x2, xnt, sct = _mix(x2d, proj, o_f.reshape(n, A_W), o_b.reshape(n, A_W), p["gn"], p["vg"],
                        p["w_sp"], p["b_sp_t"], p["w_oa"], p["w_ob"], p["w_out"], p["g2"],
                        p["w_pq"], p["keys"])
    rank1, v, nn, u = _select(sct)
    y = _peer(xnt, rank1, v, nn, u, p["pu"], p["pvt"], x2, p["gf"])
    return y.reshape(b, l, d)


def kernel(x_prompt, x_sample, norm1_g, w_in, lb_fwd, lb_bwd, gn_a, vnorm_g, w_sp, b_sp, w_oa, w_ob,
           w_out, norm2_g, w_pq, peer_keys, peer_u, peer_v, norm_f):
    layer = 0
    p = {
        "g1": norm1_g[layer].reshape(1, D_MODEL),
        "w_in": w_in[layer].astype(BF16),
        "lb_fwd": lb_fwd,
        "lb_bwd": lb_bwd,
        "gn": gn_a[layer].reshape(1, A_DV),
        "vg": vnorm_g[layer].reshape(1, B_W),
        "w_sp": w_sp[layer].astype(BF16),
        "b_sp_t": b_sp[layer].T,
        "w_oa": w_oa[layer].astype(BF16),
        "w_ob": w_ob[layer].astype(BF16),
        "w_out": w_out[layer].astype(BF16),
        "g2": norm2_g[layer].reshape(1, D_MODEL),
        "w_pq": w_pq[layer].astype(BF16),
        "keys": peer_keys[layer].reshape(2 * P_HEADS, P_NKEYS, P_DKEY).astype(BF16),
        "pu": peer_u[layer].astype(BF16),
        "pvt": peer_v[layer].astype(BF16).reshape(PEER_NI, PEER_EB, D_MODEL).transpose(0, 2, 1),
        "gf": norm_f.reshape(1, D_MODEL),
    }
    return (_trunk(x_prompt, p), _trunk(x_sample, p))
```

```python
import functools

import jax
import jax.numpy as jnp
from jax import lax
from jax.experimental import pallas as pl
from jax.experimental.pallas import tpu as pltpu

F32 = jnp.float32
BF16 = jnp.bfloat16

D_MODEL = 1024
A_HEADS = 4
A_DK = 128
A_DV = 128
A_W = A_HEADS * A_DK
A_CHUNK = 64
B_GROUPS = 4
B_GC = 128
B_W = B_GROUPS * B_GC
B_CHUNK = 128
P_HEADS = 8
P_NKEYS = 128
P_DKEY = 128
P_TOPK = 16
P_N = P_NKEYS * P_NKEYS
EPS = 1e-6
IN_COLS = 3 * A_W + 2 * A_W + 2 * B_W + 2 * D_MODEL
COL_Q, COL_FF, COL_FB, COL_I, COL_OG, COL_U, COL_V, COL_GA, COL_GB = 0, 4, 8, 12, 16, 20, 24, 28, 36

V7X_VMEM_BYTES = 64 * 1024 * 1024
VMEM_LIMIT = V7X_VMEM_BYTES - 8 * 1024 * 1024

NT_DIMS = (((1,), (1,)), ((), ()))
TN_DIMS = (((0,), (0,)), ((), ()))


GELU_C1 = 0.7978845608028654
GELU_C2 = GELU_C1 * 0.044715


def _two_gelu_tanh(x):
    t = jnp.tanh(x * (GELU_C1 + GELU_C2 * (x * x)))
    return x + x * t


def _gelu_tanh(x):
    return 0.5 * _two_gelu_tanh(x)


def _rms(x, g):
    return x * lax.rsqrt(jnp.mean(x * x, axis=-1, keepdims=True) + EPS) * g


IN_TM = 512


def _inproj_body(x_ref, g_ref, w_ref, o_ref):
    h = _rms(x_ref[...], g_ref[...])
    o_ref[...] = jnp.dot(h.astype(BF16), w_ref[...], preferred_element_type=F32)


def _inproj(x2d, g1, w_in):
    n = x2d.shape[0]
    return pl.pallas_call(
        _inproj_body,
        grid=(n // IN_TM,),
        in_specs=[
            pl.BlockSpec((IN_TM, D_MODEL), lambda i: (i, 0)),
            pl.BlockSpec((1, D_MODEL), lambda i: (0, 0)),
            pl.BlockSpec((D_MODEL, IN_COLS), lambda i: (0, 0), pipeline_mode=pl.Buffered(1)),
        ],
        out_specs=pl.BlockSpec((IN_TM, IN_COLS), lambda i: (i, 0)),
        out_shape=jax.ShapeDtypeStruct((n, IN_COLS), F32),
        compiler_params=pltpu.CompilerParams(
            dimension_semantics=("arbitrary",), vmem_limit_bytes=VMEM_LIMIT),
        name="inproj",
    )(x2d, g1, w_in)


SCAN_LT = 512
SCAN_HEADS = 4
TRI_ROWS = 256


def _first_softmax_row(lb_ref):
    lb = lb_ref[...]
    m = jnp.max(lb, axis=0, keepdims=True)
    e = jnp.exp(lb - m)
    return e[0:1, :] / jnp.sum(e, axis=0, keepdims=True)


def _scan_body(lbf_ref, lbb_ref, tri_ref, qf_ref, ff_ref, vf_ref, qb_ref, fb_ref, vb_ref,
               of_ref, ob_ref, sf_ref, sb_ref):
    @pl.when(pl.program_id(2) == 0)
    def _():
        sf_ref[...] = jnp.zeros_like(sf_ref)
        sb_ref[...] = jnp.zeros_like(sb_ref)

    n_chunks = SCAN_LT // A_CHUNK
    row = lax.broadcasted_iota(jnp.int32, (A_CHUNK, A_CHUNK), 0)
    col = lax.broadcasted_iota(jnp.int32, (A_CHUNK, A_CHUNK), 1)

    def rows(x, c):
        return x[c * A_CHUNK:(c + 1) * A_CHUNK, :]

    def prepare(q_ref, f_ref, v_ref, lb_ref, tri, last_row, lanes):
        lb = _first_softmax_row(lb_ref)[:, lanes]
        qr = q_ref[:, lanes]
        q = qr * jax.nn.sigmoid(qr)
        f = lb + (1.0 - lb) * jax.nn.sigmoid(f_ref[:, lanes])
        logf = jnp.log(f)
        k = 1.0 - f
        hi = logf.astype(BF16)
        rem = logf - hi.astype(F32)
        mid = rem.astype(BF16)
        lo = (rem - mid.astype(F32)).astype(BF16)
        terms = jnp.concatenate([hi, mid, lo], axis=-1)
        g3 = jnp.concatenate(
            [jnp.dot(tri, terms[i * TRI_ROWS:(i + 1) * TRI_ROWS, :], preferred_element_type=F32)
             for i in range(SCAN_LT // TRI_ROWS)], axis=0)
        g = (g3[:, :A_DK] + g3[:, A_DK:2 * A_DK]) + g3[:, 2 * A_DK:]
        q_dec = (q * jnp.exp(g)).astype(BF16)
        k_inv = (k * jnp.exp(-g)).astype(BF16)
        g_last = [rows(g, c)[last_row:last_row + 1, :] for c in range(n_chunks)]
        k_end = [(rows(k, c) * jnp.exp(g_last[c] - rows(g, c))).astype(BF16) for c in range(n_chunks)]
        return dict(q_dec=q_dec, k_inv=k_inv, k_end=k_end, v=v_ref[:, lanes].astype(BF16),
                    decay=[jnp.exp(gl) for gl in g_last])

    dirs = []
    for hh in range(SCAN_HEADS):
        lanes = slice(hh * A_DK, (hh + 1) * A_DK)
        dirs.append(dict(
            p=prepare(qf_ref, ff_ref, vf_ref, lbf_ref, tri_ref[0], A_CHUNK - 1, lanes),
            mask=col <= row, order=list(range(n_chunks)), o_ref=of_ref, s_ref=sf_ref,
            head=hh, lanes=lanes))
        dirs.append(dict(
            p=prepare(qb_ref, fb_ref, vb_ref, lbb_ref, tri_ref[1], 0, lanes),
            mask=col >= row, order=list(reversed(range(n_chunks))), o_ref=ob_ref, s_ref=sb_ref,
            head=hh, lanes=lanes))
    for d in dirs:
        p = d["p"]
        d["att"] = [lax.dot_general(rows(p["q_dec"], c), rows(p["k_inv"], c), NT_DIMS,
                                    preferred_element_type=F32) for c in range(n_chunks)]
        d["ds"] = [lax.dot_general(rows(p["v"], c), p["k_end"][c], TN_DIMS,
                                   preferred_element_type=F32) for c in range(n_chunks)]
    for d in dirs:
        p = d["p"]
        att = [jnp.where(d["mask"], a, 0.0).astype(BF16) for a in d["att"]]
        d["o"] = [jnp.dot(att[c], rows(p["v"], c), preferred_element_type=F32)
                  for c in range(n_chunks)]
    for d in dirs:
        p = d["p"]
        s_t = d["s_ref"][d["head"]]
        entering = {}
        for c in d["order"]:
            entering[c] = s_t.astype(BF16)
            s_t = s_t * p["decay"][c] + d["ds"][c]
        d["s_ref"][d["head"]] = s_t
        for c in range(n_chunks):
            o = d["o"][c] + lax.dot_general(rows(p["q_dec"], c), entering[c], NT_DIMS,
                                            preferred_element_type=F32)
            d["o_ref"][c * A_CHUNK:(c + 1) * A_CHUNK, d["lanes"]] = o


def _block_triangles():
    r = jnp.arange(TRI_ROWS)[:, None]
    c = jnp.arange(TRI_ROWS)[None, :]
    same = (r // A_CHUNK) == (c // A_CHUNK)
    return jnp.stack([same & (c <= r), same & (c >= r)]).astype(BF16)


def _scan(proj3, lb_fwd, lb_bwd):
    b, l, _ = proj3.shape
    nt = l // SCAN_LT
    width = SCAN_HEADS * A_DK
    blk = (None, SCAN_LT, width)

    def fwd_spec(col0):
        return pl.BlockSpec(blk, lambda bi, h, j: (bi, j, col0 // SCAN_HEADS + h))

    def bwd_spec(col0):
        return pl.BlockSpec(blk, lambda bi, h, j: (bi, nt - 1 - j, col0 // SCAN_HEADS + h))

    lb_spec = pl.BlockSpec((lb_fwd.shape[0], width), lambda bi, h, j: (0, h))
    tri_spec = pl.BlockSpec((2, TRI_ROWS, TRI_ROWS), lambda bi, h, j: (0, 0, 0))
    out_shape = jax.ShapeDtypeStruct((b, l, A_W), F32)
    return pl.pallas_call(
        _scan_body,
        grid=(b, A_HEADS // SCAN_HEADS, nt),
        in_specs=[lb_spec, lb_spec, tri_spec,
                  fwd_spec(COL_Q), fwd_spec(COL_FF), fwd_spec(COL_I),
                  bwd_spec(COL_Q), bwd_spec(COL_FB), bwd_spec(COL_I)],
        out_specs=[pl.BlockSpec(blk, lambda bi, h, j: (bi, j, h)),
                   pl.BlockSpec(blk, lambda bi, h, j: (bi, nt - 1 - j, h))],
        out_shape=[out_shape, out_shape],
        scratch_shapes=[pltpu.VMEM((SCAN_HEADS, A_DV, A_DK), F32),
                        pltpu.VMEM((SCAN_HEADS, A_DV, A_DK), F32)],
        compiler_params=pltpu.CompilerParams(
            dimension_semantics=("arbitrary", "arbitrary", "arbitrary"),
            vmem_limit_bytes=VMEM_LIMIT),
        name="gla_scan",
    )(lb_fwd, lb_bwd, _block_triangles(), proj3, proj3, proj3, proj3, proj3, proj3)


MIX_TM = 512


def _mix_body(x_ref, of_ref, ob_ref, og_ref, u_ref, v_ref, ga0_ref, ga1_ref, gb0_ref, gb1_ref,
              gn_ref, vg_ref, wsp_ref, bsp_ref, woa_ref, wob_ref, wout_ref, g2_ref, wpq_ref,
              keys_ref, x2_ref, xnt_ref, sct_ref):
    o = of_ref[...] + ob_ref[...]
    og = og_ref[...]
    gate_a = og * jax.nn.sigmoid(og)
    heads = []
    for h in range(A_HEADS):
        oh = o[:, h * A_DV:(h + 1) * A_DV]
        heads.append(_rms(oh, gn_ref[...]))
    ya_in = jnp.concatenate(heads, axis=-1) * gate_a
    ya = jnp.dot(ya_in.astype(BF16), woa_ref[...], preferred_element_type=F32)

    u = _gelu_tanh(u_ref[...])
    vv = _rms(_gelu_tanh(v_ref[...]), vg_ref[...]).astype(BF16)
    rows = []
    for c in range(MIX_TM // B_CHUNK):
        groups = []
        for g in range(B_GROUPS):
            vg_blk = vv[c * B_CHUNK:(c + 1) * B_CHUNK, g * B_GC:(g + 1) * B_GC]
            m = jnp.dot(wsp_ref[g], vg_blk, preferred_element_type=F32)
            groups.append(m + bsp_ref[:, g:g + 1])
        rows.append(jnp.concatenate(groups, axis=-1))
    mixed = jnp.concatenate(rows, axis=0)
    yb = jnp.dot((u * mixed).astype(BF16), wob_ref[...], preferred_element_type=F32)

    ga = jnp.concatenate([ga0_ref[...], ga1_ref[...]], axis=-1)
    gb = jnp.concatenate([gb0_ref[...], gb1_ref[...]], axis=-1)
    merged = jax.nn.sigmoid(ga) * ya + jax.nn.sigmoid(gb) * yb
    x2 = x_ref[...] + jnp.dot(merged.astype(BF16), wout_ref[...], preferred_element_type=F32)
    x2_ref[...] = x2

    xn = _rms(x2, g2_ref[...])
    xn_bf = xn.astype(BF16)
    xnt_ref[...] = xn.T.astype(BF16)
    qh = jnp.dot(xn_bf, wpq_ref[...], preferred_element_type=F32).astype(BF16)
    for hp in range(2 * P_HEADS):
        q_hp = qh[:, hp * P_DKEY:(hp + 1) * P_DKEY]
        sct_ref[hp] = lax.dot_general(keys_ref[hp], q_hp, NT_DIMS, preferred_element_type=F32)


def _mix(x2d, proj, o_f, o_b, gn, vg, w_sp, b_sp_t, w_oa, w_ob, w_out, g2, w_pq, keys):
    n = x2d.shape[0]
    tm = MIX_TM

    def col_spec(col0):
        return pl.BlockSpec((tm, 512), lambda i: (i, col0 // 4))

    def full(a):
        return pl.BlockSpec(a.shape, lambda i: (0,) * a.ndim)

    return pl.pallas_call(
        _mix_body,
        grid=(n // tm,),
        in_specs=[
            pl.BlockSpec((tm, D_MODEL), lambda i: (i, 0)),
            pl.BlockSpec((tm, A_W), lambda i: (i, 0)),
            pl.BlockSpec((tm, A_W), lambda i: (i, 0)),
            col_spec(COL_OG), col_spec(COL_U), col_spec(COL_V),
            col_spec(COL_GA), col_spec(COL_GA + 4), col_spec(COL_GB), col_spec(COL_GB + 4),
            full(gn), full(vg), full(w_sp), full(b_sp_t), full(w_oa), full(w_ob), full(w_out),
            full(g2), full(w_pq), full(keys),
        ],
        out_specs=[
            pl.BlockSpec((tm, D_MODEL), lambda i: (i, 0)),
            pl.BlockSpec((D_MODEL, tm), lambda i: (0, i)),
            pl.BlockSpec((2 * P_HEADS, P_NKEYS, tm), lambda i: (0, 0, i)),
        ],
        out_shape=[
            jax.ShapeDtypeStruct((n, D_MODEL), F32),
            jax.ShapeDtypeStruct((D_MODEL, n), BF16),
            jax.ShapeDtypeStruct((2 * P_HEADS, P_NKEYS, n), F32),
        ],
        compiler_params=pltpu.CompilerParams(
            dimension_semantics=("arbitrary",), vmem_limit_bytes=VMEM_LIMIT),
        name="mix",
    )(x2d, o_f, o_b, proj, proj, proj, proj, proj, proj, proj,
      gn, vg, w_sp, b_sp_t, w_oa, w_ob, w_out, g2, w_pq, keys)


SEL_TL = 256
SEL_LANES = 128
NEG_INF = float("-inf")
NOT_RANKED = float(P_TOPK + 1)
SUBLANES = 8


def _sorting_network(n):
    pairs = []
    p = 1
    while p < n:
        k = p
        while k >= 1:
            for j in range(k % p, n - k, 2 * k):
                for i in range(min(k, n - j - k)):
                    if (i + j) // (2 * p) == (i + j + k) // (2 * p):
                        pairs.append((i + j, i + j + k))
            k //= 2
        p *= 2
    return pairs


def _pop_top(levels, n_out, on_value):
    levels = list(levels)
    for i in range(n_out):
        m = jnp.max(levels[0], axis=0, keepdims=True)
        on_value(i, m)
        hit = levels[0] == m
        for k in range(min(len(levels), n_out - 1 - i)):
            below = levels[k + 1] if k + 1 < len(levels) else NEG_INF
            levels[k] = jnp.where(hit, below, levels[k])


def _top_values(s, vals_ref):
    cols = [s[k * SUBLANES:(k + 1) * SUBLANES, :] for k in range(P_NKEYS // SUBLANES)]
    for i, j in _sorting_network(len(cols)):
        cols[i], cols[j] = jnp.maximum(cols[i], cols[j]), jnp.minimum(cols[i], cols[j])

    def store(i, m):
        vals_ref[i:i + 1, :] = m

    _pop_top(cols, P_TOPK, store)


def _count_prefix(rows, holds):
    assert len(rows) == P_TOPK == 16
    t = lambda j: rows[j - 1]
    p8 = holds(t(8))
    p4 = holds(jnp.where(p8, t(12), t(4)))
    p2 = holds(jnp.where(p8, jnp.where(p4, t(14), t(10)), jnp.where(p4, t(6), t(2))))
    hi = jnp.where(p4, jnp.where(p2, t(15), t(13)), jnp.where(p2, t(11), t(9)))
    lo = jnp.where(p4, jnp.where(p2, t(7), t(5)), jnp.where(p2, t(3), t(1)))
    p1 = holds(jnp.where(p8, hi, lo))
    p16 = holds(t(16))
    one = lambda p, w: jnp.where(p, w, 0.0)
    return (one(p8, 8.0) + one(p4, 4.0)) + (one(p2, 2.0) + one(p1, 1.0)) + one(p16, 1.0)


def _select_body(sct_ref, rank_ref, v_ref, n_ref, u_ref, top_ref):
    for lg in range(SEL_TL // SEL_LANES):
        lanes = slice(lg * SEL_LANES, (lg + 1) * SEL_LANES)
        _select_lanes(sct_ref, rank_ref, v_ref, n_ref, u_ref, top_ref.at[lg, 0], top_ref.at[lg, 1],
                      lanes)


def _select_lanes(sct_ref, rank_ref, v_ref, n_ref, u_ref, v0_ref, v1_ref, lanes):
    for h in range(P_HEADS):
        s0 = sct_ref[2 * h, :, lanes]
        s1 = sct_ref[2 * h + 1, :, lanes]
        _top_values(s0, v0_ref)
        _top_values(s1, v1_ref)
        top0 = v0_ref[...]
        top1 = v1_ref[...]
        best = top0[0:1, :] + top1[0:1, :]
        lo_levels = [top0[0:SUBLANES, :] + top1[j:j + 1, :] for j in range(P_TOPK)]
        hi_level0 = top0[SUBLANES:, :] + top1[0:1, :]
        levels = [jnp.concatenate([lo_levels[0], hi_level0], axis=0)] + [
            jnp.concatenate([lv, jnp.full_like(lv, NEG_INF)], axis=0) for lv in lo_levels[1:]]
        stats = {"z": jnp.zeros_like(best), "tau": best}

        def accumulate(i, m, stats=stats, best=best):
            stats["z"] = stats["z"] + jnp.exp(m - best)
            stats["tau"] = m

        _pop_top(levels, P_TOPK, accumulate)
        tau = stats["tau"]
        rows1 = [top1[j:j + 1, :] for j in range(P_TOPK)]
        n = _count_prefix(rows1, lambda t: s0 + t >= tau)
        rank1 = _count_prefix(rows1, lambda t: t > s1) + 1.0
        rank_ref[h, :, lanes] = rank1.astype(BF16)
        v_ref[h, :, lanes] = jnp.exp(s1 - top1[0:1, :]).astype(BF16)
        n_ref[h, :, lanes] = n
        u_ref[h, :, lanes] = jnp.exp(s0 - top0[0:1, :]) * (0.5 / stats["z"])


def _select(sct):
    n = sct.shape[-1]
    tl = SEL_TL
    spec = pl.BlockSpec((P_HEADS, P_NKEYS, tl), lambda i: (0, 0, i))
    shape16 = jax.ShapeDtypeStruct((P_HEADS, P_NKEYS, n), BF16)
    shape32 = jax.ShapeDtypeStruct((P_HEADS, P_NKEYS, n), F32)
    return pl.pallas_call(
        _select_body,
        grid=(n // tl,),
        in_specs=[pl.BlockSpec((2 * P_HEADS, P_NKEYS, tl), lambda i: (0, 0, i))],
        out_specs=[spec, spec, spec, spec],
        out_shape=[shape16, shape16, shape32, shape32],
        scratch_shapes=[pltpu.VMEM((tl // SEL_LANES, 2, P_TOPK, SEL_LANES), F32)],
        compiler_params=pltpu.CompilerParams(
            dimension_semantics=("arbitrary",), vmem_limit_bytes=VMEM_LIMIT),
        name="select",
    )(sct)


PEER_T = 512
PEER_EB = 2048
PEER_NI = P_N // PEER_EB
PEER_LANES = 256
BF16_ROWS = 16


def _gate_tile(rank_ref, v_ref, n_rows, u_rows, lanes):
    groups = P_NKEYS // BF16_ROWS
    width = lanes.stop - lanes.start
    g = jnp.zeros((groups, BF16_ROWS, width), BF16)
    for h in range(P_HEADS):
        nb = jnp.broadcast_to(n_rows[h][:, lanes], (BF16_ROWS, width)).astype(BF16)
        ub = jnp.broadcast_to(u_rows[h][:, lanes], (BF16_ROWS, width)).astype(BF16)
        r = rank_ref[h, :, lanes].reshape(groups, BF16_ROWS, width)
        vv = v_ref[h, :, lanes].reshape(groups, BF16_ROWS, width)
        g = g + jnp.where(r <= nb[None], vv, jnp.zeros_like(vv)) * ub[None]
    return g.reshape(P_NKEYS, width)


def _peer_body(xnt_ref, rank_ref, v_ref, n_ref, u_ref, pu_ref, pvt_ref, x2_ref, gf_ref,
               y_ref, acc_ref, w_ref, *, n_blocks):
    s = pl.program_id(0)
    i1 = jnp.minimum(s, n_blocks - 1) % PEER_NI
    i3 = jnp.clip(s - 1, 0, n_blocks - 1) % PEER_NI
    cur = s % 2
    prev = 1 - cur

    @pl.when(s == 0)
    def _():
        w_ref[...] = jnp.zeros_like(w_ref)

    @pl.when(i3 == 0)
    def _():
        acc_ref[...] = jnp.zeros_like(acc_ref)

    act = jnp.dot(pu_ref[...], xnt_ref[...], preferred_element_type=F32)
    acc_ref[...] += jnp.dot(pvt_ref[0], w_ref[prev], preferred_element_type=F32)

    for al in range(PEER_EB // P_NKEYS):
        a = i1 * (PEER_EB // P_NKEYS) + al
        rows = slice(al * P_NKEYS, (al + 1) * P_NKEYS)
        gelu = _two_gelu_tanh(act[rows, :].astype(BF16))
        n_rows = [n_ref[h, pl.ds(a, 1), :] for h in range(P_HEADS)]
        u_rows = [u_ref[h, pl.ds(a, 1), :] for h in range(P_HEADS)]
        for lh in range(PEER_T // PEER_LANES):
            lanes = slice(lh * PEER_LANES, (lh + 1) * PEER_LANES)
            w_ref[cur, rows, lanes] = (
                _gate_tile(rank_ref, v_ref, n_rows, u_rows, lanes) * gelu[:, lanes])

    @pl.when(jnp.logical_and(s >= 1, i3 == PEER_NI - 1))
    def _():
        y_ref[...] = _rms(x2_ref[...] + acc_ref[...].T, gf_ref[...])


def _peer(xnt, rank1, v, nn, u, pu, pvt3, x2, gf):
    n = x2.shape[0]
    t, eb = PEER_T, PEER_EB
    n_blocks = (n // t) * PEER_NI
    last = n_blocks - 1

    def pair(s, lag):
        return jnp.clip(s - lag, 0, last)

    def sel_spec():
        return pl.BlockSpec((P_HEADS, P_NKEYS, t), lambda s: (0, 0, pair(s, 0) // PEER_NI))

    return pl.pallas_call(
        functools.partial(_peer_body, n_blocks=n_blocks),
        grid=(n_blocks + 1,),
        in_specs=[
            pl.BlockSpec((D_MODEL, t), lambda s: (0, pair(s, 0) // PEER_NI)),
            sel_spec(), sel_spec(), sel_spec(), sel_spec(),
            pl.BlockSpec((eb, D_MODEL), lambda s: (pair(s, 0) % PEER_NI, 0)),
            pl.BlockSpec((1, D_MODEL, eb), lambda s: (pair(s, 1) % PEER_NI, 0, 0)),
            pl.BlockSpec((t, D_MODEL), lambda s: (pair(s, 1) // PEER_NI, 0)),
            pl.BlockSpec((1, D_MODEL), lambda s: (0, 0)),
        ],
        out_specs=pl.BlockSpec((t, D_MODEL), lambda s: (pair(s, 1) // PEER_NI, 0)),
        out_shape=jax.ShapeDtypeStruct((n, D_MODEL), F32),
        scratch_shapes=[pltpu.VMEM((D_MODEL, t), F32),
                        pltpu.VMEM((2, eb, t), BF16)],
        compiler_params=pltpu.CompilerParams(
            dimension_semantics=("arbitrary",), vmem_limit_bytes=VMEM_LIMIT),
        name="peer",
    )(xnt, rank1, v, nn, u, pu, pvt3, x2, gf)


def _trunk(x, p):
    b, l, d = x.shape
    n = b * l
    x2d = x.reshape(n, d)
    proj = _inproj(x2d, p["g1"], p["w_in"])
    o_f, o_b = _scan(proj.reshape(b, l, IN_COLS), p["lb_fwd"], p["lb_bwd"])
    x2, xnt, sct = _mix(x2d, proj, o_f.reshape(n, A_W), o_b.reshape(n, A_W), p["gn"], p["vg"],
                        p["w_sp"], p["b_sp_t"], p["w_oa"], p["w_ob"], p["w_out"], p["g2"],
                        p["w_pq"], p["keys"])
    rank1, v, nn, u = _select(sct)
    y = _peer(xnt, rank1, v, nn, u, p["pu"], p["pvt"], x2, p["gf"])
    return y.reshape(b, l, d)


def kernel(x_prompt, x_sample, norm1_g, w_in, lb_fwd, lb_bwd, gn_a, vnorm_g, w_sp, b_sp, w_oa, w_ob,
           w_out, norm2_g, w_pq, peer_keys, peer_u, peer_v, norm_f):
    layer = 0
    p = {
        "g1": norm1_g[layer].reshape(1, D_MODEL),
        "w_in": w_in[layer].astype(BF16),
        "lb_fwd": lb_fwd,
        "lb_bwd": lb_bwd,
        "gn": gn_a[layer].reshape(1, A_DV),
        "vg": vnorm_g[layer].reshape(1, B_W),
        "w_sp": w_sp[layer].astype(BF16),
        "b_sp_t": b_sp[layer].T,
        "w_oa": w_oa[layer].astype(BF16),
        "w_ob": w_ob[layer].astype(BF16),
        "w_out": w_out[layer].astype(BF16),
        "g2": norm2_g[layer].reshape(1, D_MODEL),
        "w_pq": w_pq[layer].astype(BF16),
        "keys": peer_keys[layer].reshape(2 * P_HEADS, P_NKEYS, P_DKEY).astype(BF16),
        "pu": peer_u[layer].astype(BF16),
        "pvt": peer_v[layer].astype(BF16).reshape(PEER_NI, PEER_EB, D_MODEL).transpose(0, 2, 1),
        "gf": norm_f.reshape(1, D_MODEL),
    }
    return (_trunk(x_prompt, p), _trunk(x_sample, p))
```

```python
import functools

import jax
import jax.numpy as jnp
from jax import lax
from jax.experimental import pallas as pl
from jax.experimental.pallas import tpu as pltpu

F32 = jnp.float32
BF16 = jnp.bfloat16

D_MODEL = 1024
A_HEADS = 4
A_DK = 128
A_DV = 128
A_W = A_HEADS * A_DK
A_CHUNK = 64
B_GROUPS = 4
B_GC = 128
B_W = B_GROUPS * B_GC
B_CHUNK = 128
P_HEADS = 8
P_NKEYS = 128
P_DKEY = 128
P_TOPK = 16
P_N = P_NKEYS * P_NKEYS
EPS = 1e-6
IN_COLS = 3 * A_W + 2 * A_W + 2 * B_W + 2 * D_MODEL
COL_Q, COL_FF, COL_FB, COL_I, COL_OG, COL_U, COL_V, COL_GA, COL_GB = 0, 4, 8, 12, 16, 20, 24, 28, 36

V7X_VMEM_BYTES = 64 * 1024 * 1024
VMEM_LIMIT = V7X_VMEM_BYTES - 8 * 1024 * 1024

NT_DIMS = (((1,), (1,)), ((), ()))
TN_DIMS = (((0,), (0,)), ((), ()))


GELU_C1 = 0.7978845608028654
GELU_C2 = GELU_C1 * 0.044715


def _two_gelu_tanh(x):
    t = jnp.tanh(x * (GELU_C1 + GELU_C2 * (x * x)))
    return x + x * t


def _gelu_tanh(x):
    return 0.5 * _two_gelu_tanh(x)


def _rms(x, g):
    return x * lax.rsqrt(jnp.mean(x * x, axis=-1, keepdims=True) + EPS) * g


IN_TM = 512


def _inproj_body(x_ref, g_ref, w_ref, o_ref):
    h = _rms(x_ref[...], g_ref[...])
    o_ref[...] = jnp.dot(h.astype(BF16), w_ref[...], preferred_element_type=F32)


def _inproj(x2d, g1, w_in):
    n = x2d.shape[0]
    return pl.pallas_call(
        _inproj_body,
        grid=(n // IN_TM,),
        in_specs=[
            pl.BlockSpec((IN_TM, D_MODEL), lambda i: (i, 0)),
            pl.BlockSpec((1, D_MODEL), lambda i: (0, 0)),
            pl.BlockSpec((D_MODEL, IN_COLS), lambda i: (0, 0), pipeline_mode=pl.Buffered(1)),
        ],
        out_specs=pl.BlockSpec((IN_TM, IN_COLS), lambda i: (i, 0)),
        out_shape=jax.ShapeDtypeStruct((n, IN_COLS), F32),
        compiler_params=pltpu.CompilerParams(
            dimension_semantics=("arbitrary",), vmem_limit_bytes=VMEM_LIMIT),
        name="inproj",
    )(x2d, g1, w_in)


SCAN_LT = 512
SCAN_HEADS = 4
TRI_ROWS = 256


def _first_softmax_row(lb_ref):
    lb = lb_ref[...]
    m = jnp.max(lb, axis=0, keepdims=True)
    e = jnp.exp(lb - m)
    return e[0:1, :] / jnp.sum(e, axis=0, keepdims=True)


def _scan_body(lbf_ref, lbb_ref, tri_ref, qf_ref, ff_ref, vf_ref, qb_ref, fb_ref, vb_ref,
               of_ref, ob_ref, sf_ref, sb_ref):
    @pl.when(pl.program_id(2) == 0)
    def _():
        sf_ref[...] = jnp.zeros_like(sf_ref)
        sb_ref[...] = jnp.zeros_like(sb_ref)

    n_chunks = SCAN_LT // A_CHUNK
    row = lax.broadcasted_iota(jnp.int32, (A_CHUNK, A_CHUNK), 0)
    col = lax.broadcasted_iota(jnp.int32, (A_CHUNK, A_CHUNK), 1)

    def rows(x, c):
        return x[c * A_CHUNK:(c + 1) * A_CHUNK, :]

    def prepare(q_ref, f_ref, v_ref, lb_ref, tri, last_row, lanes):
        lb = _first_softmax_row(lb_ref)[:, lanes]
        qr = q_ref[:, lanes]
        q = qr * jax.nn.sigmoid(qr)
        f = lb + (1.0 - lb) * jax.nn.sigmoid(f_ref[:, lanes])
        logf = jnp.log(f)
        k = 1.0 - f
        hi = logf.astype(BF16)
        rem = logf - hi.astype(F32)
        mid = rem.astype(BF16)
        lo = (rem - mid.astype(F32)).astype(BF16)
        terms = jnp.concatenate([hi, mid, lo], axis=-1)
        g3 = jnp.concatenate(
            [jnp.dot(tri, terms[i * TRI_ROWS:(i + 1) * TRI_ROWS, :], preferred_element_type=F32)
             for i in range(SCAN_LT // TRI_ROWS)], axis=0)
        g = (g3[:, :A_DK] + g3[:, A_DK:2 * A_DK]) + g3[:, 2 * A_DK:]
        q_dec = (q * jnp.exp(g)).astype(BF16)
        k_inv = (k * jnp.exp(-g)).astype(BF16)
        g_last = [rows(g, c)[last_row:last_row + 1, :] for c in range(n_chunks)]
        k_end = [(rows(k, c) * jnp.exp(g_last[c] - rows(g, c))).astype(BF16) for c in range(n_chunks)]
        return dict(q_dec=q_dec, k_inv=k_inv, k_end=k_end, v=v_ref[:, lanes].astype(BF16),
                    decay=[jnp.exp(gl) for gl in g_last])

    dirs = []
    for hh in range(SCAN_HEADS):
        lanes = slice(hh * A_DK, (hh + 1) * A_DK)
        dirs.append(dict(
            p=prepare(qf_ref, ff_ref, vf_ref, lbf_ref, tri_ref[0], A_CHUNK - 1, lanes),
            mask=col <= row, order=list(range(n_chunks)), o_ref=of_ref, s_ref=sf_ref,
            head=hh, lanes=lanes))
        dirs.append(dict(
            p=prepare(qb_ref, fb_ref, vb_ref, lbb_ref, tri_ref[1], 0, lanes),
            mask=col >= row, order=list(reversed(range(n_chunks))), o_ref=ob_ref, s_ref=sb_ref,
            head=hh, lanes=lanes))
    for d in dirs:
        p = d["p"]
        d["att"] = [lax.dot_general(rows(p["q_dec"], c), rows(p["k_inv"], c), NT_DIMS,
                                    preferred_element_type=F32) for c in range(n_chunks)]
        d["ds"] = [lax.dot_general(rows(p["v"], c), p["k_end"][c], TN_DIMS,
                                   preferred_element_type=F32) for c in range(n_chunks)]
    for d in dirs:
        p = d["p"]
        att = [jnp.where(d["mask"], a, 0.0).astype(BF16) for a in d["att"]]
        d["o"] = [jnp.dot(att[c], rows(p["v"], c), preferred_element_type=F32)
                  for c in range(n_chunks)]
    for d in dirs:
        p = d["p"]
        s_t = d["s_ref"][d["head"]]
        entering = {}
        for c in d["order"]:
            entering[c] = s_t.astype(BF16)
            s_t = s_t * p["decay"][c] + d["ds"][c]
        d["s_ref"][d["head"]] = s_t
        for c in range(n_chunks):
            o = d["o"][c] + lax.dot_general(rows(p["q_dec"], c), entering[c], NT_DIMS,
                                            preferred_element_type=F32)
            d["o_ref"][c * A_CHUNK:(c + 1) * A_CHUNK, d["lanes"]] = o


def _block_triangles():
    r = jnp.arange(TRI_ROWS)[:, None]
    c = jnp.arange(TRI_ROWS)[None, :]
    same = (r // A_CHUNK) == (c // A_CHUNK)
    return jnp.stack([same & (c <= r), same & (c >= r)]).astype(BF16)


def _scan(proj3, lb_fwd, lb_bwd):
    b, l, _ = proj3.shape
    nt = l // SCAN_LT
    width = SCAN_HEADS * A_DK
    blk = (None, SCAN_LT, width)

    def fwd_spec(col0):
        return pl.BlockSpec(blk, lambda bi, h, j: (bi, j, col0 // SCAN_HEADS + h))

    def bwd_spec(col0):
        return pl.BlockSpec(blk, lambda bi, h, j: (bi, nt - 1 - j, col0 // SCAN_HEADS + h))

    lb_spec = pl.BlockSpec((lb_fwd.shape[0], width), lambda bi, h, j: (0, h))
    tri_spec = pl.BlockSpec((2, TRI_ROWS, TRI_ROWS), lambda bi, h, j: (0, 0, 0))
    out_shape = jax.ShapeDtypeStruct((b, l, A_W), F32)
    return pl.pallas_call(
        _scan_body,
        grid=(b, A_HEADS // SCAN_HEADS, nt),
        in_specs=[lb_spec, lb_spec, tri_spec,
                  fwd_spec(COL_Q), fwd_spec(COL_FF), fwd_spec(COL_I),
                  bwd_spec(COL_Q), bwd_spec(COL_FB), bwd_spec(COL_I)],
        out_specs=[pl.BlockSpec(blk, lambda bi, h, j: (bi, j, h)),
                   pl.BlockSpec(blk, lambda bi, h, j: (bi, nt - 1 - j, h))],
        out_shape=[out_shape, out_shape],
        scratch_shapes=[pltpu.VMEM((SCAN_HEADS, A_DV, A_DK), F32),
                        pltpu.VMEM((SCAN_HEADS, A_DV, A_DK), F32)],
        compiler_params=pltpu.CompilerParams(
            dimension_semantics=("arbitrary", "arbitrary", "arbitrary"),
            vmem_limit_bytes=VMEM_LIMIT),
        name="gla_scan",
    )(lb_fwd, lb_bwd, _block_triangles(), proj3, proj3, proj3, proj3, proj3, proj3)


MIX_TM = 512


def _mix_body(x_ref, of_ref, ob_ref, og_ref, u_ref, v_ref, ga0_ref, ga1_ref, gb0_ref, gb1_ref,
              gn_ref, vg_ref, wsp_ref, bsp_ref, woa_ref, wob_ref, wout_ref, g2_ref,
              x2_ref, xnt_ref):
    o = of_ref[...] + ob_ref[...]
    og = og_ref[...]
    gate_a = og * jax.nn.sigmoid(og)
    heads = []
    for h in range(A_HEADS):
        oh = o[:, h * A_DV:(h + 1) * A_DV]
        heads.append(_rms(oh, gn_ref[...]))
    ya_in = jnp.concatenate(heads, axis=-1) * gate_a
    ya = jnp.dot(ya_in.astype(BF16), woa_ref[...], preferred_element_type=F32)

    u = _gelu_tanh(u_ref[...])
    vv = _rms(_gelu_tanh(v_ref[...]), vg_ref[...]).astype(BF16)
    rows = []
    for c in range(MIX_TM // B_CHUNK):
        groups = []
        for g in range(B_GROUPS):
            vg_blk = vv[c * B_CHUNK:(c + 1) * B_CHUNK, g * B_GC:(g + 1) * B_GC]
            m = jnp.dot(wsp_ref[g], vg_blk, preferred_element_type=F32)
            groups.append(m + bsp_ref[:, g:g + 1])
        rows.append(jnp.concatenate(groups, axis=-1))
    mixed = jnp.concatenate(rows, axis=0)
    yb = jnp.dot((u * mixed).astype(BF16), wob_ref[...], preferred_element_type=F32)

    ga = jnp.concatenate([ga0_ref[...], ga1_ref[...]], axis=-1)
    gb = jnp.concatenate([gb0_ref[...], gb1_ref[...]], axis=-1)
    merged = jax.nn.sigmoid(ga) * ya + jax.nn.sigmoid(gb) * yb
    x2 = x_ref[...] + jnp.dot(merged.astype(BF16), wout_ref[...], preferred_element_type=F32)
    x2_ref[...] = x2

    xn = _rms(x2, g2_ref[...])
    xnt_ref[...] = xn.T.astype(BF16)


def _mix(x2d, proj, o_f, o_b, gn, vg, w_sp, b_sp_t, w_oa, w_ob, w_out, g2):
    n = x2d.shape[0]
    tm = MIX_TM

    def col_spec(col0):
        return pl.BlockSpec((tm, 512), lambda i: (i, col0 // 4))

    def full(a):
        return pl.BlockSpec(a.shape, lambda i: (0,) * a.ndim)

    return pl.pallas_call(
        _mix_body,
        grid=(n // tm,),
        in_specs=[
            pl.BlockSpec((tm, D_MODEL), lambda i: (i, 0)),
            pl.BlockSpec((tm, A_W), lambda i: (i, 0)),
            pl.BlockSpec((tm, A_W), lambda i: (i, 0)),
            col_spec(COL_OG), col_spec(COL_U), col_spec(COL_V),
            col_spec(COL_GA), col_spec(COL_GA + 4), col_spec(COL_GB), col_spec(COL_GB + 4),
            full(gn), full(vg), full(w_sp), full(b_sp_t), full(w_oa), full(w_ob), full(w_out),
            full(g2),
        ],
        out_specs=[
            pl.BlockSpec((tm, D_MODEL), lambda i: (i, 0)),
            pl.BlockSpec((D_MODEL, tm), lambda i: (0, i)),
        ],
        out_shape=[
            jax.ShapeDtypeStruct((n, D_MODEL), F32),
            jax.ShapeDtypeStruct((D_MODEL, n), BF16),
        ],
        compiler_params=pltpu.CompilerParams(
            dimension_semantics=("arbitrary",), vmem_limit_bytes=VMEM_LIMIT),
        name="mix",
    )(x2d, o_f, o_b, proj, proj, proj, proj, proj, proj, proj,
      gn, vg, w_sp, b_sp_t, w_oa, w_ob, w_out, g2)


SEL_TL = 128
SEL_LANES = 128
NEG_INF = float("-inf")
NOT_RANKED = float(P_TOPK + 1)
SUBLANES = 8


def _sorting_network(n):
    pairs = []
    p = 1
    while p < n:
        k = p
        while k >= 1:
            for j in range(k % p, n - k, 2 * k):
                for i in range(min(k, n - j - k)):
                    if (i + j) // (2 * p) == (i + j + k) // (2 * p):
                        pairs.append((i + j, i + j + k))
            k //= 2
        p *= 2
    return pairs


def _pop_top(levels, n_out, on_value):
    levels = list(levels)
    for i in range(n_out):
        m = jnp.max(levels[0], axis=0, keepdims=True)
        on_value(i, m)
        hit = levels[0] == m
        for k in range(min(len(levels), n_out - 1 - i)):
            below = levels[k + 1] if k + 1 < len(levels) else NEG_INF
            levels[k] = jnp.where(hit, below, levels[k])


def _top_values(s, vals_ref):
    cols = [s[k * SUBLANES:(k + 1) * SUBLANES, :] for k in range(P_NKEYS // SUBLANES)]
    for i, j in _sorting_network(len(cols)):
        cols[i], cols[j] = jnp.maximum(cols[i], cols[j]), jnp.minimum(cols[i], cols[j])

    def store(i, m):
        vals_ref[i:i + 1, :] = m

    _pop_top(cols, P_TOPK, store)


def _count_prefix(rows, holds):
    assert len(rows) == P_TOPK == 16
    t = lambda j: rows[j - 1]
    p8 = holds(t(8))
    p4 = holds(jnp.where(p8, t(12), t(4)))
    p2 = holds(jnp.where(p8, jnp.where(p4, t(14), t(10)), jnp.where(p4, t(6), t(2))))
    hi = jnp.where(p4, jnp.where(p2, t(15), t(13)), jnp.where(p2, t(11), t(9)))
    lo = jnp.where(p4, jnp.where(p2, t(7), t(5)), jnp.where(p2, t(3), t(1)))
    p1 = holds(jnp.where(p8, hi, lo))
    p16 = holds(t(16))
    one = lambda p, w: jnp.where(p, w, 0.0)
    return (one(p8, 8.0) + one(p4, 4.0)) + (one(p2, 2.0) + one(p1, 1.0)) + one(p16, 1.0)


def _select_body(xnt_ref, wpqt_ref, keys_ref, rank_ref, v_ref, n_ref, u_ref, v0_ref, v1_ref):
    xnt = xnt_ref[...]
    for h in range(P_HEADS):
        qt = jnp.dot(wpqt_ref[h * 2 * P_DKEY:(h + 1) * 2 * P_DKEY, :], xnt,
                     preferred_element_type=F32).astype(BF16)
        s0 = jnp.dot(keys_ref[2 * h], qt[:P_DKEY], preferred_element_type=F32)
        s1 = jnp.dot(keys_ref[2 * h + 1], qt[P_DKEY:], preferred_element_type=F32)
        for lg in range(SEL_TL // SEL_LANES):
            lanes = slice(lg * SEL_LANES, (lg + 1) * SEL_LANES)
            _select_lanes(s0[:, lanes], s1[:, lanes], h, lanes, rank_ref, v_ref, n_ref, u_ref,
                          v0_ref, v1_ref)


def _select_lanes(s0, s1, h, lanes, rank_ref, v_ref, n_ref, u_ref, v0_ref, v1_ref):
    _top_values(s0, v0_ref)
    _top_values(s1, v1_ref)
    top0 = v0_ref[...]
    top1 = v1_ref[...]
    best = top0[0:1, :] + top1[0:1, :]
    lo_levels = [top0[0:SUBLANES, :] + top1[j:j + 1, :] for j in range(P_TOPK)]
    hi_level0 = top0[SUBLANES:, :] + top1[0:1, :]
    levels = [jnp.concatenate([lo_levels[0], hi_level0], axis=0)] + [
        jnp.concatenate([lv, jnp.full_like(lv, NEG_INF)], axis=0) for lv in lo_levels[1:]]
    stats = {"z": jnp.zeros_like(best), "tau": best}

    def accumulate(i, m):
        stats["z"] = stats["z"] + jnp.exp(m - best)
        stats["tau"] = m

    _pop_top(levels, P_TOPK, accumulate)
    tau = stats["tau"]
    rows1 = [top1[j:j + 1, :] for j in range(P_TOPK)]
    n = _count_prefix(rows1, lambda t: s0 + t >= tau)
    rank1 = _count_prefix(rows1, lambda t: t > s1) + 1.0
    rank_ref[h, :, lanes] = rank1.astype(BF16)
    v_ref[h, :, lanes] = jnp.exp(s1 - top1[0:1, :]).astype(BF16)
    n_ref[h, :, lanes] = n
    u_ref[h, :, lanes] = jnp.exp(s0 - top0[0:1, :]) * (0.5 / stats["z"])


def _select(xnt, w_pqt, keys):
    n = xnt.shape[-1]
    tl = SEL_TL
    spec = pl.BlockSpec((P_HEADS, P_NKEYS, tl), lambda i: (0, 0, i))
    shape16 = jax.ShapeDtypeStruct((P_HEADS, P_NKEYS, n), BF16)
    shape32 = jax.ShapeDtypeStruct((P_HEADS, P_NKEYS, n), F32)
    return pl.pallas_call(
        _select_body,
        grid=(n // tl,),
        in_specs=[
            pl.BlockSpec((D_MODEL, tl), lambda i: (0, i)),
            pl.BlockSpec(w_pqt.shape, lambda i: (0, 0), pipeline_mode=pl.Buffered(1)),
            pl.BlockSpec(keys.shape, lambda i: (0, 0, 0), pipeline_mode=pl.Buffered(1)),
        ],
        out_specs=[spec, spec, spec, spec],
        out_shape=[shape16, shape16, shape32, shape32],
        scratch_shapes=[pltpu.VMEM((P_TOPK, SEL_LANES), F32), pltpu.VMEM((P_TOPK, SEL_LANES), F32)],
        compiler_params=pltpu.CompilerParams(
            dimension_semantics=("arbitrary",), vmem_limit_bytes=VMEM_LIMIT),
        name="select",
    )(xnt, w_pqt, keys)


PEER_T = 512
PEER_EB = 2048
PEER_NI = P_N // PEER_EB
PEER_LANES = 256
BF16_ROWS = 16


def _gate_tile(rank_ref, v_ref, n_rows, u_rows, lanes):
    groups = P_NKEYS // BF16_ROWS
    width = lanes.stop - lanes.start
    g = jnp.zeros((groups, BF16_ROWS, width), BF16)
    for h in range(P_HEADS):
        nb = jnp.broadcast_to(n_rows[h][:, lanes], (BF16_ROWS, width)).astype(BF16)
        ub = jnp.broadcast_to(u_rows[h][:, lanes], (BF16_ROWS, width)).astype(BF16)
        r = rank_ref[h, :, lanes].reshape(groups, BF16_ROWS, width)
        vv = v_ref[h, :, lanes].reshape(groups, BF16_ROWS, width)
        g = g + jnp.where(r <= nb[None], vv, jnp.zeros_like(vv)) * ub[None]
    return g.reshape(P_NKEYS, width)


def _peer_body(xnt_ref, rank_ref, v_ref, n_ref, u_ref, pu_ref, pvt_ref, x2_ref, gf_ref,
               y_ref, acc_ref, w_ref, *, n_blocks):
    s = pl.program_id(0)
    i1 = jnp.minimum(s, n_blocks - 1) % PEER_NI
    i3 = jnp.clip(s - 1, 0, n_blocks - 1) % PEER_NI
    cur = s % 2
    prev = 1 - cur

    @pl.when(s == 0)
    def _():
        w_ref[...] = jnp.zeros_like(w_ref)

    @pl.when(i3 == 0)
    def _():
        acc_ref[...] = jnp.zeros_like(acc_ref)

    act = jnp.dot(pu_ref[...], xnt_ref[...], preferred_element_type=F32)
    acc_ref[...] += jnp.dot(pvt_ref[0], w_ref[prev], preferred_element_type=F32)

    for al in range(PEER_EB // P_NKEYS):
        a = i1 * (PEER_EB // P_NKEYS) + al
        rows = slice(al * P_NKEYS, (al + 1) * P_NKEYS)
        gelu = _two_gelu_tanh(act[rows, :].astype(BF16))
        n_rows = [n_ref[h, pl.ds(a, 1), :] for h in range(P_HEADS)]
        u_rows = [u_ref[h, pl.ds(a, 1), :] for h in range(P_HEADS)]
        for lh in range(PEER_T // PEER_LANES):
            lanes = slice(lh * PEER_LANES, (lh + 1) * PEER_LANES)
            w_ref[cur, rows, lanes] = (
                _gate_tile(rank_ref, v_ref, n_rows, u_rows, lanes) * gelu[:, lanes])

    @pl.when(jnp.logical_and(s >= 1, i3 == PEER_NI - 1))
    def _():
        y_ref[...] = _rms(x2_ref[...] + acc_ref[...].T, gf_ref[...])


def _peer(xnt, rank1, v, nn, u, pu, pvt3, x2, gf):
    n = x2.shape[0]
    t, eb = PEER_T, PEER_EB
    n_blocks = (n // t) * PEER_NI
    last = n_blocks - 1

    def pair(s, lag):
        return jnp.clip(s - lag, 0, last)

    def sel_spec():
        return pl.BlockSpec((P_HEADS, P_NKEYS, t), lambda s: (0, 0, pair(s, 0) // PEER_NI))

    return pl.pallas_call(
        functools.partial(_peer_body, n_blocks=n_blocks),
        grid=(n_blocks + 1,),
        in_specs=[
            pl.BlockSpec((D_MODEL, t), lambda s: (0, pair(s, 0) // PEER_NI)),
            sel_spec(), sel_spec(), sel_spec(), sel_spec(),
            pl.BlockSpec((eb, D_MODEL), lambda s: (pair(s, 0) % PEER_NI, 0)),
            pl.BlockSpec((1, D_MODEL, eb), lambda s: (pair(s, 1) % PEER_NI, 0, 0)),
            pl.BlockSpec((t, D_MODEL), lambda s: (pair(s, 1) // PEER_NI, 0)),
            pl.BlockSpec((1, D_MODEL), lambda s: (0, 0)),
        ],
        out_specs=pl.BlockSpec((t, D_MODEL), lambda s: (pair(s, 1) // PEER_NI, 0)),
        out_shape=jax.ShapeDtypeStruct((n, D_MODEL), F32),
        scratch_shapes=[pltpu.VMEM((D_MODEL, t), F32),
                        pltpu.VMEM((2, eb, t), BF16)],
        compiler_params=pltpu.CompilerParams(
            dimension_semantics=("arbitrary",), vmem_limit_bytes=VMEM_LIMIT),
        name="peer",
    )(xnt, rank1, v, nn, u, pu, pvt3, x2, gf)


def _trunk(x, p):
    b, l, d = x.shape
    n = b * l
    x2d = x.reshape(n, d)
    proj = _inproj(x2d, p["g1"], p["w_in"])
    o_f, o_b = _scan(proj.reshape(b, l, IN_COLS), p["lb_fwd"], p["lb_bwd"])
    x2, xnt = _mix(x2d, proj, o_f.reshape(n, A_W), o_b.reshape(n, A_W), p["gn"], p["vg"],
                   p["w_sp"], p["b_sp_t"], p["w_oa"], p["w_ob"], p["w_out"], p["g2"])
    rank1, v, nn, u = _select(xnt, p["w_pqt"], p["keys"])
    y = _peer(xnt, rank1, v, nn, u, p["pu"], p["pvt"], x2, p["gf"])
    return y.reshape(b, l, d)


def kernel(x_prompt, x_sample, norm1_g, w_in, lb_fwd, lb_bwd, gn_a, vnorm_g, w_sp, b_sp, w_oa, w_ob,
           w_out, norm2_g, w_pq, peer_keys, peer_u, peer_v, norm_f):
    layer = 0
    p = {
        "g1": norm1_g[layer].reshape(1, D_MODEL),
        "w_in": w_in[layer].astype(BF16),
        "lb_fwd": lb_fwd,
        "lb_bwd": lb_bwd,
        "gn": gn_a[layer].reshape(1, A_DV),
        "vg": vnorm_g[layer].reshape(1, B_W),
        "w_sp": w_sp[layer].astype(BF16),
        "b_sp_t": b_sp[layer].T,
        "w_oa": w_oa[layer].astype(BF16),
        "w_ob": w_ob[layer].astype(BF16),
        "w_out": w_out[layer].astype(BF16),
        "g2": norm2_g[layer].reshape(1, D_MODEL),
        "w_pqt": w_pq[layer].astype(BF16).T,
        "keys": peer_keys[layer].reshape(2 * P_HEADS, P_NKEYS, P_DKEY).astype(BF16),
        "pu": peer_u[layer].astype(BF16),
        "pvt": peer_v[layer].astype(BF16).reshape(PEER_NI, PEER_EB, D_MODEL).transpose(0, 2, 1),
        "gf": norm_f.reshape(1, D_MODEL),
    }
    return (_trunk(x_prompt, p), _trunk(x_sample, p))
```
